```python
import math
import jax, jax.numpy as jnp
from jax import lax
import numpy as np


D_MODEL = 1024
BATCH = 4
SEQ = 4096
DEPTH = 4

HEAD_DIM = 64
A_GROUPS = 4
A_WIDTH = A_GROUPS * HEAD_DIM
CHUNK = 128
B_HEADS = 6
B_WIDTH = B_HEADS * HEAD_DIM
DIL_PAIRS = ((128, 1), (512, 4), (2048, 16))
C_HEADS_PER_GROUP = 2
C_GROUPS = len(DIL_PAIRS)
C_HEADS = C_GROUPS * C_HEADS_PER_GROUP
C_WIDTH = C_HEADS * HEAD_DIM
C_DILATIONS = tuple(d for (w, d) in DIL_PAIRS for _ in range(C_HEADS_PER_GROUP))
N_OFFSETS = DIL_PAIRS[0][0] // DIL_PAIRS[0][1] + 1
MIX_WIDTH = A_WIDTH + B_WIDTH + C_WIDTH
IN_WIDTH = 2 * A_WIDTH + 3 * B_WIDTH + 3 * C_WIDTH
Q_BLOCK = 128
D_FF = 2816
CONV_WIDTH = 3
ROPE_THETA = 10000.0
EPS = 1e-6
N_MOD = 6

kernel_name = 'hybrid_sgu_stickbreak_dilated_convffn_adaln'


def _rmsnorm(x, g):
    x32 = x.astype(jnp.float32)
    y = x32 * lax.rsqrt(jnp.mean(x32 * x32, axis=-1, keepdims=True) + EPS)
    return (y * g.astype(jnp.float32)).astype(x.dtype)


def _rope_tables(positions):
    inv_freq = ROPE_THETA ** (-jnp.arange(0, HEAD_DIM, 2, dtype=jnp.float32) / HEAD_DIM)
    ang = positions.astype(jnp.float32)[..., None] * inv_freq
    return jnp.cos(ang)[:, :, None, :], jnp.sin(ang)[:, :, None, :]


def _rope(x, cos, sin):
    x32 = x.astype(jnp.float32)
    x1, x2 = jnp.split(x32, 2, axis=-1)
    return jnp.concatenate([x1 * cos - x2 * sin, x2 * cos + x1 * sin], axis=-1).astype(x.dtype)


def _spatial_gating(u, v, g_sgu, w_sp, b_sp):
    bsz, seq, _ = u.shape
    n_chunks = seq // CHUNK
    u = jax.nn.gelu(u)
    v = jax.nn.gelu(v).reshape(bsz, n_chunks, CHUNK, A_GROUPS, HEAD_DIM)
    v = _rmsnorm(v, g_sgu.reshape(A_GROUPS, HEAD_DIM))
    causal = jnp.tril(jnp.ones((CHUNK, CHUNK), dtype=bool))
    w_causal = jnp.where(causal[None], w_sp, 0.0).astype(v.dtype)
    mixed = jnp.einsum('gts,bnsgc->bntgc', w_causal, v) + b_sp.T[:, :, None].astype(v.dtype)
    return u * mixed.reshape(bsz, seq, A_WIDTH)


def _stick_breaking(q, k, v):
    bsz, seq, n_heads, dh = q.shape
    n_blocks = seq // Q_BLOCK
    scale = dh ** -0.5
    k_pos = jnp.arange(seq)
    q_blocks = q.reshape(bsz, n_blocks, Q_BLOCK, n_heads, dh).transpose(1, 0, 2, 3, 4)

    def one_block(args):
        q_blk, blk = args
        q_pos = blk * Q_BLOCK + jnp.arange(Q_BLOCK)
        before = (k_pos[None, :] < q_pos[:, None])[None, None]
        z = jnp.einsum('bqhd,bkhd->bhqk', q_blk, k).astype(jnp.float32) * scale
        log_beta = jax.nn.log_sigmoid(z)
        log_stay = jnp.where(before, jax.nn.log_sigmoid(-z), 0.0)
        tail = lax.cumsum(log_stay, axis=3, reverse=True) - log_stay
        weight = jnp.where(before, jnp.exp(log_beta + tail), 0.0)
        return jnp.einsum('bhqk,bkhd->bqhd', weight.astype(v.dtype), v)

    out = lax.map(one_block, (q_blocks, jnp.arange(n_blocks)))
    return out.transpose(1, 0, 2, 3, 4).reshape(bsz, seq, n_heads * dh)


def _dilated_window(q, k, v):
    bsz, seq, n_heads, dh = q.shape
    n_blocks = seq // Q_BLOCK
    scale = dh ** -0.5
    dil = jnp.array(C_DILATIONS, dtype=jnp.int32)
    offsets = jnp.arange(N_OFFSETS, dtype=jnp.int32)
    head_idx = jnp.arange(n_heads)[None, None, :]
    q_blocks = q.reshape(bsz, n_blocks, Q_BLOCK, n_heads, dh).transpose(1, 0, 2, 3, 4)

    def one_block(args):
        q_blk, blk = args
        q_pos = blk * Q_BLOCK + jnp.arange(Q_BLOCK, dtype=jnp.int32)
        idx = q_pos[:, None, None] - offsets[None, :, None] * dil[None, None, :]
        valid = idx >= 0
        idx = jnp.maximum(idx, 0)
        k_g = k[:, idx, head_idx, :]
        v_g = v[:, idx, head_idx, :]
        z = jnp.einsum('bqhd,bqmhd->bqhm', q_blk, k_g).astype(jnp.float32) * scale
        z = jnp.where(valid.transpose(0, 2, 1)[None], z, -jnp.inf)
        z_max = jnp.max(z, axis=-1, keepdims=True)
        p = jnp.exp(z - z_max)
        denom = jnp.sum(p, axis=-1)
        o = jnp.einsum('bqhm,bqmhd->bqhd', p.astype(v.dtype), v_g) / denom[..., None]
        return o.astype(q.dtype), z_max[..., 0] + jnp.log(denom)

    o, lse = lax.map(one_block, (q_blocks, jnp.arange(n_blocks)))
    o = o.transpose(1, 0, 2, 3, 4).reshape(bsz, seq, C_GROUPS, C_HEADS_PER_GROUP, dh)
    lse = lse.transpose(1, 0, 2, 3).reshape(bsz, seq, C_GROUPS, C_HEADS_PER_GROUP)
    alpha = jax.nn.softmax(lse, axis=2)
    return (o * alpha[..., None].astype(o.dtype)).reshape(bsz, seq, n_heads * dh)


def _causal_dwconv(h, w, b):
    seq = h.shape[1]
    hp = jnp.pad(h, ((0, 0), (CONV_WIDTH - 1, 0), (0, 0)))
    out = b.astype(h.dtype)
    for i in range(CONV_WIDTH):
        out = out + w[i].astype(h.dtype) * hp[:, i:i + seq]
    return out


def setup_inputs(seed: int = 0) -> dict:
    key = jax.random.key(seed)
    ks = jax.random.split(key, 17)
    f32 = jnp.float32

    def nrm(k, shape, s):
        return jax.random.normal(k, shape, f32) * s

    x = nrm(ks[0], (BATCH, SEQ, D_MODEL), 1.0)
    c = nrm(ks[1], (BATCH, D_MODEL), 1.0)
    offset = jax.random.randint(ks[2], (BATCH, 1), 0, 1024, dtype=jnp.int32)
    positions = jnp.arange(SEQ, dtype=jnp.int32)[None, :] + offset
    w_ada = nrm(ks[3], (DEPTH, D_MODEL, N_MOD * D_MODEL), 0.2 * D_MODEL ** -0.5)
    b_ada = nrm(ks[4], (DEPTH, N_MOD * D_MODEL), 0.02)
    g_mix = 1.0 + nrm(ks[5], (DEPTH, D_MODEL), 0.05)
    w_in = nrm(ks[6], (DEPTH, D_MODEL, IN_WIDTH), D_MODEL ** -0.5)
    g_sgu = 1.0 + nrm(ks[7], (DEPTH, A_WIDTH), 0.05)
    w_sp = nrm(ks[8], (DEPTH, A_GROUPS, CHUNK, CHUNK), CHUNK ** -0.5)
    b_sp = 1.0 + nrm(ks[9], (DEPTH, A_GROUPS, CHUNK), 0.1)
    w_out = nrm(ks[10], (DEPTH, MIX_WIDTH, D_MODEL), MIX_WIDTH ** -0.5)
    g_ffn = 1.0 + nrm(ks[11], (DEPTH, D_MODEL), 0.05)
    w_up = nrm(ks[12], (DEPTH, D_MODEL, 2 * D_FF), D_MODEL ** -0.5)
    conv_w = nrm(ks[13], (DEPTH, CONV_WIDTH, 2 * D_FF), CONV_WIDTH ** -0.5)
    conv_b = nrm(ks[14], (DEPTH, 2 * D_FF), 0.02)
    w_down = nrm(ks[15], (DEPTH, D_FF, D_MODEL), D_FF ** -0.5)
    g_final = 1.0 + nrm(ks[16], (D_MODEL,), 0.05)
    return {'x': x, 'c': c, 'positions': positions, 'w_ada': w_ada, 'b_ada': b_ada,
            'g_mix': g_mix, 'w_in': w_in, 'g_sgu': g_sgu, 'w_sp': w_sp, 'b_sp': b_sp,
            'w_out': w_out, 'g_ffn': g_ffn, 'w_up': w_up, 'conv_w': conv_w,
            'conv_b': conv_b, 'w_down': w_down, 'g_final': g_final}


def reference(x, c, positions, w_ada, b_ada, g_mix, w_in, g_sgu, w_sp, b_sp, w_out,
              g_ffn, w_up, conv_w, conv_b, w_down, g_final):
    bsz, seq, _ = x.shape
    cos, sin = _rope_tables(positions)
    splits = np.cumsum([A_WIDTH, A_WIDTH, B_WIDTH, B_WIDTH, B_WIDTH, C_WIDTH, C_WIDTH]).tolist()
    c_act = jax.nn.silu(c)
    for l in range(DEPTH):
        mod = (c_act @ w_ada[l] + b_ada[l])[:, None, :]
        shift1, scale1, gate1, shift2, scale2, gate2 = jnp.split(mod, N_MOD, axis=-1)
        h = _rmsnorm(x, g_mix[l]) * (1.0 + scale1) + shift1
        proj = h @ w_in[l]
        a_u, a_v, b_q, b_k, b_v, c_q, c_k, c_v = jnp.split(proj, splits, axis=-1)
        y_a = _spatial_gating(a_u, a_v, g_sgu[l], w_sp[l], b_sp[l])
        y_b = _stick_breaking(b_q.reshape(bsz, seq, B_HEADS, HEAD_DIM),
                              b_k.reshape(bsz, seq, B_HEADS, HEAD_DIM),
                              b_v.reshape(bsz, seq, B_HEADS, HEAD_DIM))
        y_c = _dilated_window(_rope(c_q.reshape(bsz, seq, C_HEADS, HEAD_DIM), cos, sin),
                              _rope(c_k.reshape(bsz, seq, C_HEADS, HEAD_DIM), cos, sin),
                              c_v.reshape(bsz, seq, C_HEADS, HEAD_DIM))
        y = jnp.concatenate([y_a, y_b, y_c], axis=-1) @ w_out[l]
        x = x + (1.0 + gate1) * y
        h = _rmsnorm(x, g_ffn[l]) * (1.0 + scale2) + shift2
        up = _causal_dwconv(h @ w_up[l], conv_w[l], conv_b[l])
        gate, val = jnp.split(up, 2, axis=-1)
        x = x + (1.0 + gate2) * ((jax.nn.silu(gate) * val) @ w_down[l])
    return _rmsnorm(x, g_final)
```

```python
import functools

import jax
import jax.numpy as jnp
from jax import lax
from jax.experimental import pallas as pl
from jax.experimental.pallas import tpu as pltpu

F32 = jnp.float32
BF16 = jnp.bfloat16

LANES = 128
HEAD_DIM = 64
HEADS_PER_TILE = LANES // HEAD_DIM
A_GROUPS = 4
A_WIDTH = A_GROUPS * HEAD_DIM
CHUNK = 128
B_WIDTH = 6 * HEAD_DIM
C_WIDTH = 6 * HEAD_DIM
DILATIONS = (1, 4, 16)
WINDOW = 128
Q_BLOCK = 128
CONV_WIDTH = 3
CONV_HALO = 8
ROPE_THETA = 10000.0
EPS = 1e-6
N_MOD = 6
EXP_ZERO_BELOW = -104.0
VMEM_LIMIT_BYTES = 56 * 1024 * 1024


def _params(*semantics):
    return pltpu.CompilerParams(dimension_semantics=semantics,
                                vmem_limit_bytes=VMEM_LIMIT_BYTES)


def _split_bf16(a):
    hi = a.astype(BF16)
    lo = (a - hi.astype(F32)).astype(BF16)
    return hi, lo


def _rms_scale(x):
    return x * lax.rsqrt(jnp.mean(x * x, axis=-1, keepdims=True) + EPS)


def _mod_kernel(c_ref, w_ref, b_ref, o_ref):
    c = c_ref[...]
    c_act = c * (1.0 / (1.0 + jnp.exp(-c)))
    o_ref[0] = jnp.dot(c_act, w_ref[0], preferred_element_type=F32) + b_ref[0]


def _modulation(c, w_ada, b_ada):
    depth, d, nd = w_ada.shape
    bsz = c.shape[0]
    rows = -(-bsz // 8) * 8
    c_pad = jnp.zeros((rows, d), F32).at[:bsz].set(c)
    out = pl.pallas_call(
        _mod_kernel,
        grid=(depth, nd // d),
        in_specs=[pl.BlockSpec((rows, d), lambda l, j: (0, 0)),
                  pl.BlockSpec((1, d, d), lambda l, j: (l, 0, j)),
                  pl.BlockSpec((1, 1, d), lambda l, j: (l, 0, j))],
        out_specs=pl.BlockSpec((1, rows, d), lambda l, j: (l, 0, j)),
        out_shape=jax.ShapeDtypeStruct((depth, rows, nd), F32),
        compiler_params=_params("parallel", "parallel"),
    )(c_pad, w_ada, b_ada.reshape(depth, 1, nd))
    return out[:, :bsz].reshape(depth, bsz, N_MOD, d)


def _rope_kernel(pos_ref, freq_ref, sign_ref, cos_ref, sin_ref):
    ang = pos_ref[0] * freq_ref[...]
    cos_ref[0] = jnp.cos(ang)
    sin_ref[0] = jnp.sin(ang) * sign_ref[...]


def _rope_tables(positions):
    bsz, seq = positions.shape
    half = HEAD_DIM // 2
    inv_freq = ROPE_THETA ** (-jnp.arange(0, HEAD_DIM, 2, dtype=F32) / HEAD_DIM)
    freq = jnp.tile(inv_freq, LANES // half).reshape(1, LANES)
    sign = jnp.tile(jnp.concatenate([-jnp.ones((half,), F32), jnp.ones((half,), F32)]),
                    HEADS_PER_TILE).reshape(1, LANES)
    pos = positions.astype(F32).reshape(bsz, seq, 1)
    row = pl.BlockSpec((1, LANES), lambda b: (0, 0))
    tab = pl.BlockSpec((1, seq, LANES), lambda b: (b, 0, 0))
    return pl.pallas_call(
        _rope_kernel,
        grid=(bsz,),
        in_specs=[pl.BlockSpec((1, seq, 1), lambda b: (b, 0, 0)), row, row],
        out_specs=[tab, tab],
        out_shape=[jax.ShapeDtypeStruct((bsz, seq, LANES), F32)] * 2,
        compiler_params=_params("parallel"),
    )(pos, freq, sign)


def _inproj_kernel(x_ref, mod_ref, g_ref, w_ref, cos_ref, sin_ref, gsgu_ref, wsp_ref, bsp_ref,
                   ya_ref, bq_ref, bk_ref, bv_ref, cq_ref, ck_ref, cv_ref, h_scr, *, tm):
    mod = mod_ref[0]
    h = _rms_scale(x_ref[...]) * g_ref[...] * (1.0 + mod[1:2]) + mod[0:1]
    h_scr[...] = h.astype(BF16)
    scale = HEAD_DIM ** -0.5

    def proj(c0, width):
        return jnp.dot(h_scr[...], w_ref[:, c0:c0 + width], preferred_element_type=F32)

    pa = proj(0, 2 * A_WIDTH)
    u = jax.nn.gelu(pa[:, :A_WIDTH])
    v = jax.nn.gelu(pa[:, A_WIDTH:])
    grp_r = lax.broadcasted_iota(jnp.int32, (A_WIDTH, A_WIDTH), 0) // HEAD_DIM
    grp_c = lax.broadcasted_iota(jnp.int32, (A_WIDTH, A_WIDTH), 1) // HEAD_DIM
    group_ones = jnp.where(grp_r == grp_c, 1.0, 0.0).astype(BF16)
    hi, lo = _split_bf16(v * v)
    ssq = (jnp.dot(hi, group_ones, preferred_element_type=F32)
           + jnp.dot(lo, group_ones, preferred_element_type=F32))
    vn = v * lax.rsqrt(ssq * (1.0 / HEAD_DIM) + EPS) * gsgu_ref[...]
    t_idx = lax.broadcasted_iota(jnp.int32, (CHUNK, A_GROUPS * CHUNK), 0)
    s_idx = lax.broadcasted_iota(jnp.int32, (CHUNK, A_GROUPS * CHUNK), 1) % CHUNK
    w_causal = jnp.where(s_idx <= t_idx, wsp_ref[...], 0.0).astype(BF16)
    lane_grp = lax.broadcasted_iota(jnp.int32, (CHUNK, A_WIDTH), 1) // HEAD_DIM
    for c in range(tm // CHUNK):
        rows = slice(c * CHUNK, (c + 1) * CHUNK)
        vc = vn[rows]
        stacked = jnp.concatenate(
            [jnp.where(lane_grp == g, vc, 0.0) for g in range(A_GROUPS)], axis=0).astype(BF16)
        mixed = jnp.dot(w_causal, stacked, preferred_element_type=F32) + bsp_ref[...]
        ya_ref[rows, :] = (u[rows] * mixed).astype(BF16)

    c0 = 2 * A_WIDTH
    bq_ref[...] = (proj(c0, B_WIDTH) * scale).astype(BF16)
    bk_ref[...] = proj(c0 + B_WIDTH, B_WIDTH).astype(BF16)
    bv_ref[...] = proj(c0 + 2 * B_WIDTH, B_WIDTH).astype(BF16)

    cos = cos_ref[...]
    sin = sin_ref[...]
    first_half = (lax.broadcasted_iota(jnp.int32, (tm, LANES), 1) % HEAD_DIM) < HEAD_DIM // 2

    def rope(pc):
        outs = []
        for p in range(C_WIDTH // LANES):
            xp = pc[:, p * LANES:(p + 1) * LANES]
            rot = jnp.where(first_half,
                            pltpu.roll(xp, LANES - HEAD_DIM // 2, axis=1),
                            pltpu.roll(xp, HEAD_DIM // 2, axis=1))
            outs.append(xp * cos + rot * sin)
        return jnp.concatenate(outs, axis=1)

    c0 += 3 * B_WIDTH
    cq_ref[...] = (rope(proj(c0, C_WIDTH)) * scale).astype(BF16)
    ck_ref[...] = rope(proj(c0 + C_WIDTH, C_WIDTH)).astype(BF16)
    cv_ref[...] = proj(c0 + 2 * C_WIDTH, C_WIDTH).astype(BF16)


def _in_projection(x2d, mod_l, g_mix, w_in, cos, sin, g_sgu, w_sp, b_sp, *, seq, tm):
    tokens, d = x2d.shape
    per_batch = seq // tm
    in_width = w_in.shape[1]
    w_cat = jnp.transpose(w_sp, (1, 0, 2)).reshape(CHUNK, A_GROUPS * CHUNK)
    bias = jnp.repeat(b_sp.T, HEAD_DIM, axis=1)
    tile = lambda w: pl.BlockSpec((tm, w), lambda i: (i, 0))
    full = lambda a: pl.BlockSpec(a.shape, lambda i: (0,) * a.ndim)
    g_mix = g_mix.reshape(1, d)
    g_sgu = g_sgu.reshape(1, A_WIDTH)
    outs = pl.pallas_call(
        functools.partial(_inproj_kernel, tm=tm),
        grid=(tokens // tm,),
        in_specs=[tile(d),
                  pl.BlockSpec((1, N_MOD, d), lambda i: (i // per_batch, 0, 0)),
                  full(g_mix), full(w_in), tile(LANES), tile(LANES),
                  full(g_sgu), full(w_cat), full(bias)],
        out_specs=[tile(A_WIDTH)] + [tile(B_WIDTH)] * 3 + [tile(C_WIDTH)] * 3,
        out_shape=[jax.ShapeDtypeStruct((tokens, w), BF16)
                   for w in (A_WIDTH,) + (B_WIDTH,) * 3 + (C_WIDTH,) * 3],
        scratch_shapes=[pltpu.VMEM((tm, d), BF16)],
        compiler_params=_params("parallel"),
    )(x2d, mod_l, g_mix, w_in, cos, sin, g_sgu, w_cat, bias)
    assert in_width == 2 * A_WIDTH + 3 * B_WIDTH + 3 * C_WIDTH
    return outs


def _stick_kernel(q_ref, k_ref, v_ref, o_ref, acc_scr, carry_scr, *, blk):
    i = pl.program_id(2)
    lane_head = lax.broadcasted_iota(jnp.int32, (blk, LANES), 1) // HEAD_DIM
    q = q_ref[...]
    qh = [jnp.where(lane_head == h, q, jnp.zeros_like(q)) for h in range(HEADS_PER_TILE)]
    j_idx = lax.broadcasted_iota(jnp.int32, (blk, 2 * blk), 0)
    s_idx = lax.broadcasted_iota(jnp.int32, (blk, 2 * blk), 1)
    later = jnp.where((j_idx > s_idx) | (s_idx >= blk), 1.0, 0.0).astype(BF16)
    t_idx = lax.broadcasted_iota(jnp.int32, (blk, blk), 0)
    before = lax.broadcasted_iota(jnp.int32, (blk, blk), 1) < t_idx

    acc_scr[...] = jnp.zeros_like(acc_scr)
    carry_scr[...] = jnp.zeros_like(carry_scr)

    def visit(start, diagonal):
        k_blk = k_ref[pl.ds(start, blk), :]
        v_blk = v_ref[pl.ds(start, blk), :]
        for h in range(HEADS_PER_TILE):
            z = lax.dot_general(qh[h], k_blk, (((1,), (1,)), ((), ())),
                                preferred_element_type=F32)
            soft = jnp.log1p(jnp.exp(-jnp.abs(z)))
            log_beta = jnp.minimum(z, 0.0) - soft
            log_stay = -jnp.maximum(z, 0.0) - soft
            if diagonal:
                log_stay = jnp.where(before, log_stay, 0.0)
            hi, lo = _split_bf16(log_stay)
            sums = (jnp.dot(hi, later, preferred_element_type=F32)
                    + jnp.dot(lo, later, preferred_element_type=F32))
            carry = carry_scr[h]
            weight = jnp.exp(log_beta + sums[:, :blk] + carry)
            if diagonal:
                weight = jnp.where(before, weight, 0.0)
            acc_scr[h] += jnp.dot(weight.astype(BF16), v_blk, preferred_element_type=F32)
            carry_scr[h] = carry + sums[:, blk:]

    def alive():
        top = jnp.max(jnp.max(carry_scr[...], axis=0), axis=0, keepdims=True)
        return (top[0, 0] > EXP_ZERO_BELOW).astype(jnp.int32)

    visit(pl.multiple_of(i * blk, blk), True)

    def cond(state):
        kb, go = state
        return jnp.logical_and(kb >= 0, go > 0)

    def body(state):
        kb, _ = state
        visit(pl.multiple_of(kb * blk, blk), False)
        return kb - 1, alive()

    lax.while_loop(cond, body, (i - 1, alive()))
    o_ref[...] = jnp.where(lane_head == 0, acc_scr[0], acc_scr[1]).astype(BF16)


def _stick_breaking(q, k, v, *, bsz, seq):
    tokens, width = q.shape
    blk = Q_BLOCK
    nq = seq // blk
    q_spec = pl.BlockSpec((blk, LANES), lambda b, p, i: (b * nq + i, p))
    kv_spec = pl.BlockSpec((seq, LANES), lambda b, p, i: (b, p))
    return pl.pallas_call(
        functools.partial(_stick_kernel, blk=blk),
        grid=(bsz, width // LANES, nq),
        in_specs=[q_spec, kv_spec, kv_spec],
        out_specs=q_spec,
        out_shape=jax.ShapeDtypeStruct((tokens, width), BF16),
        scratch_shapes=[pltpu.VMEM((HEADS_PER_TILE, blk, LANES), F32),
                        pltpu.VMEM((HEADS_PER_TILE, blk, blk), F32)],
        compiler_params=_params("parallel", "parallel", "parallel"),
    )(q, k, v)


def _window_kernel(q_ref, kp_ref, k_ref, vp_ref, v_ref, o_ref, lse_ref, *, tq, class_blocks):
    p = pl.program_id(1)
    i = pl.program_id(2)
    sub = tq // WINDOW
    n_class = jnp.int32(class_blocks[-1])
    for idx in range(len(class_blocks) - 1):
        n_class = jnp.where(p == idx, jnp.int32(class_blocks[idx]), n_class)
    lane_head = lax.broadcasted_iota(jnp.int32, (WINDOW, LANES), 1) // HEAD_DIM
    r_idx = lax.broadcasted_iota(jnp.int32, (WINDOW, 2 * WINDOW), 0)
    c_idx = lax.broadcasted_iota(jnp.int32, (WINDOW, 2 * WINDOW), 1)
    off = c_idx - r_idx
    band = (off >= 0) & (off <= WINDOW)
    for j in range(sub):
        rows = slice(j * WINDOW, (j + 1) * WINDOW)
        prev = slice((j - 1) * WINDOW, j * WINDOW)
        q = q_ref[0, 0, rows, :]
        k_cat = jnp.concatenate([kp_ref[0, 0] if j == 0 else k_ref[0, 0, prev, :],
                                 k_ref[0, 0, rows, :]], axis=0)
        v_cat = jnp.concatenate([vp_ref[0, 0] if j == 0 else v_ref[0, 0, prev, :],
                                 v_ref[0, 0, rows, :]], axis=0)
        has_prev = ((i * sub + j) % n_class) != 0
        valid = band & ((c_idx >= WINDOW) | has_prev)
        o_heads, lse_heads = [], []
        for h in range(HEADS_PER_TILE):
            qh = jnp.where(lane_head == h, q, jnp.zeros_like(q))
            z = lax.dot_general(qh, k_cat, (((1,), (1,)), ((), ())),
                                preferred_element_type=F32)
            z = jnp.where(valid, z, -jnp.inf)
            z_max = jnp.max(z, axis=-1, keepdims=True)
            prob = jnp.exp(z - z_max)
            denom = jnp.sum(prob, axis=-1, keepdims=True)
            o_h = jnp.dot(prob.astype(BF16), v_cat, preferred_element_type=F32) * (1.0 / denom)
            o_heads.append(o_h)
            lse_heads.append(z_max + jnp.log(denom))
        o_ref[0, 0, rows, :] = jnp.where(lane_head == 0, o_heads[0], o_heads[1]).astype(BF16)
        lse_ref[0, 0, rows, :] = jnp.where(lane_head == 0, lse_heads[0], lse_heads[1])


def _dilated_window(q, k, v, *, tq):
    bsz, pairs, seq, _ = q.shape
    sub = tq // WINDOW
    class_blocks = tuple(seq // d // WINDOW for d in DILATIONS)
    cur = pl.BlockSpec((1, 1, tq, LANES), lambda b, p, i: (b, p, i, 0))
    prv = pl.BlockSpec((1, 1, WINDOW, LANES),
                       lambda b, p, i: (b, p, jnp.maximum(i * sub - 1, 0), 0))
    return pl.pallas_call(
        functools.partial(_window_kernel, tq=tq, class_blocks=class_blocks),
        grid=(bsz, pairs, seq // tq),
        in_specs=[cur, prv, cur, prv, cur],
        out_specs=[cur, cur],
        out_shape=[jax.ShapeDtypeStruct(q.shape, BF16), jax.ShapeDtypeStruct(q.shape, F32)],
        compiler_params=_params("parallel", "parallel", "parallel"),
    )(q, k, k, v, v)


def _to_classes(a, bsz, seq):
    a = a.reshape(bsz, seq, len(DILATIONS), LANES)
    parts = []
    for p, d in enumerate(DILATIONS):
        part = a[:, :, p, :]
        if d > 1:
            part = part.reshape(bsz, seq // d, d, LANES).transpose(0, 2, 1, 3).reshape(bsz, seq, LANES)
        parts.append(part)
    return jnp.stack(parts, axis=1)


def _from_classes(a):
    bsz, pairs, seq, _ = a.shape
    parts = []
    for p, d in enumerate(DILATIONS):
        part = a[:, p]
        if d > 1:
            part = part.reshape(bsz, d, seq // d, LANES).transpose(0, 2, 1, 3).reshape(bsz, seq, LANES)
        parts.append(part)
    return jnp.concatenate(parts, axis=-1).reshape(bsz * seq, pairs * LANES)


def _outproj_kernel(x_ref, mod_ref, ya_ref, yb_ref, oc_ref, lse_ref, w_ref, o_ref, y_scr):
    y_scr[:, :A_WIDTH] = ya_ref[...]
    y_scr[:, A_WIDTH:A_WIDTH + B_WIDTH] = yb_ref[...]
    pairs = C_WIDTH // LANES
    lse = [lse_ref[:, p * LANES:(p + 1) * LANES] for p in range(pairs)]
    top = functools.reduce(jnp.maximum, lse)
    share = [jnp.exp(l - top) for l in lse]
    inv = 1.0 / functools.reduce(lambda a, b: a + b, share)
    c0 = A_WIDTH + B_WIDTH
    for p in range(pairs):
        o_p = oc_ref[:, p * LANES:(p + 1) * LANES].astype(F32)
        y_scr[:, c0 + p * LANES:c0 + (p + 1) * LANES] = (o_p * (share[p] * inv)).astype(BF16)
    y = jnp.dot(y_scr[...], w_ref[...], preferred_element_type=F32)
    o_ref[...] = x_ref[...] + (1.0 + mod_ref[0][2:3]) * y


def _out_projection(x2d, mod_l, y_a, y_b, o_c, lse, w_out, *, seq, tm):
    tokens, d = x2d.shape
    per_batch = seq // tm
    tile = lambda w: pl.BlockSpec((tm, w), lambda i: (i, 0))
    return pl.pallas_call(
        _outproj_kernel,
        grid=(tokens // tm,),
        in_specs=[tile(d),
                  pl.BlockSpec((1, N_MOD, d), lambda i: (i // per_batch, 0, 0)),
                  tile(A_WIDTH), tile(B_WIDTH), tile(C_WIDTH), tile(C_WIDTH),
                  pl.BlockSpec(w_out.shape, lambda i: (0, 0))],
        out_specs=tile(d),
        out_shape=jax.ShapeDtypeStruct((tokens, d), F32),
        scratch_shapes=[pltpu.VMEM((tm, w_out.shape[0]), BF16)],
        compiler_params=_params("parallel"),
    )(x2d, mod_l, y_a, y_b, o_c, lse, w_out)


def _ffn_kernel(x_ref, halo_ref, mod_ref, g_ref, wup_ref, cw_ref, cb_ref, wdn_ref, gfin_ref,
                o_ref, h_scr, acc_scr, *, tm, tn, per_batch, final):
    i = pl.program_id(0)
    mod = mod_ref[0]
    d_ff = wdn_ref.shape[0]

    def norm_mod(xv):
        return _rms_scale(xv) * g_ref[...] * (1.0 + mod[4:5]) + mod[3:4]

    halo = jnp.where(i % per_batch == 0, 0.0, norm_mod(halo_ref[...]))
    h_scr[:CONV_HALO, :] = halo.astype(BF16)
    h_scr[CONV_HALO:, :] = norm_mod(x_ref[...]).astype(BF16)
    acc_scr[...] = jnp.zeros_like(acc_scr)

    def conv(c0):
        up = jnp.dot(h_scr[...], wup_ref[:, c0:c0 + tn], preferred_element_type=F32)
        w = cw_ref[:, c0:c0 + tn]
        out = cb_ref[:, c0:c0 + tn]
        for tap in range(CONV_WIDTH):
            shift = CONV_HALO - (CONV_WIDTH - 1) + tap
            out = out + w[tap:tap + 1] * up[shift:shift + tm]
        return out

    for n in range(d_ff // tn):
        gate = conv(n * tn)
        val = conv(d_ff + n * tn)
        act = gate * (1.0 / (1.0 + jnp.exp(-gate))) * val
        acc_scr[...] += jnp.dot(act.astype(BF16), wdn_ref[n * tn:(n + 1) * tn, :],
                                preferred_element_type=F32)
    x_new = x_ref[...] + (1.0 + mod[5:6]) * acc_scr[...]
    if final:
        x_new = _rms_scale(x_new) * gfin_ref[...]
    o_ref[...] = x_new


def _channel_mixer(x2d, mod_l, g_ffn, w_up, conv_w, conv_b, w_down, g_final, *, seq, tm, tn, final):
    tokens, d = x2d.shape
    per_batch = seq // tm
    halo_blocks = tm // CONV_HALO
    full = lambda a: pl.BlockSpec(a.shape, lambda i: (0,) * a.ndim)
    resident = lambda a: pl.BlockSpec(a.shape, lambda i: (0,) * a.ndim,
                                      pipeline_mode=pl.Buffered(1))
    g_ffn = g_ffn.reshape(1, d)
    conv_b = conv_b.reshape(1, -1)
    g_final = g_final.reshape(1, d)
    return pl.pallas_call(
        functools.partial(_ffn_kernel, tm=tm, tn=tn, per_batch=per_batch, final=final),
        grid=(tokens // tm,),
        in_specs=[pl.BlockSpec((tm, d), lambda i: (i, 0)),
                  pl.BlockSpec((CONV_HALO, d), lambda i: (jnp.maximum(i * halo_blocks - 1, 0), 0)),
                  pl.BlockSpec((1, N_MOD, d), lambda i: (i // per_batch, 0, 0)),
                  full(g_ffn), resident(w_up), full(conv_w), full(conv_b), resident(w_down),
                  full(g_final)],
        out_specs=pl.BlockSpec((tm, d), lambda i: (i, 0)),
        out_shape=jax.ShapeDtypeStruct((tokens, d), F32),
        scratch_shapes=[pltpu.VMEM((tm + CONV_HALO, d), BF16), pltpu.VMEM((tm, d), F32)],
        compiler_params=_params("parallel"),
    )(x2d, x2d, mod_l, g_ffn, w_up, conv_w, conv_b, w_down, g_final)


def kernel(x, c, positions, w_ada, b_ada, g_mix, w_in, g_sgu, w_sp, b_sp, w_out, g_ffn, w_up,
           conv_w, conv_b, w_down, g_final):
    bsz, seq, d = x.shape
    depth = w_ada.shape[0]
    tm = min(512, seq)
    assert seq % tm == 0 and seq % (max(DILATIONS) * WINDOW) == 0
    mod = _modulation(c, w_ada, b_ada)
    cos, sin = _rope_tables(positions)
    cos = cos.reshape(bsz * seq, LANES)
    sin = sin.reshape(bsz * seq, LANES)
    x2d = x.reshape(bsz * seq, d)
    for l in range(depth):
        y_a, b_q, b_k, b_v, c_q, c_k, c_v = _in_projection(
            x2d, mod[l], g_mix[l], w_in[l].astype(BF16), cos, sin, g_sgu[l], w_sp[l], b_sp[l],
            seq=seq, tm=tm)
        y_b = _stick_breaking(b_q, b_k, b_v, bsz=bsz, seq=seq)
        o_c, lse = _dilated_window(_to_classes(c_q, bsz, seq), _to_classes(c_k, bsz, seq),
                                   _to_classes(c_v, bsz, seq), tq=2 * WINDOW)
        x2d = _out_projection(x2d, mod[l], y_a, y_b, _from_classes(o_c), _from_classes(lse),
                              w_out[l].astype(BF16), seq=seq, tm=tm)
        x2d = _channel_mixer(x2d, mod[l], g_ffn[l], w_up[l].astype(BF16), conv_w[l], conv_b[l],
                             w_down[l].astype(BF16), g_final, seq=seq, tm=tm, tn=256,
                             final=(l == depth - 1))
    return x2d.reshape(bsz, seq, d)
```

```python
import functools

import jax
import jax.numpy as jnp
from jax import lax
from jax.experimental import pallas as pl
from jax.experimental.pallas import tpu as pltpu

F32 = jnp.float32
BF16 = jnp.bfloat16

LANES = 128
HEAD_DIM = 64
HEADS_PER_TILE = LANES // HEAD_DIM
A_GROUPS = 4
A_WIDTH = A_GROUPS * HEAD_DIM
CHUNK = 128
B_WIDTH = 6 * HEAD_DIM
C_WIDTH = 6 * HEAD_DIM
DILATIONS = (1, 4, 16)
WINDOW = 128
Q_BLOCK = 128
STICK_LEAD_BLOCKS = 3
STICK_Q_BLOCKS_PER_STEP = 4
CONV_WIDTH = 3
CONV_HALO = 8
ROPE_THETA = 10000.0
EPS = 1e-6
N_MOD = 6
EXP_ZERO_BELOW = -104.0
VMEM_LIMIT_BYTES = 56 * 1024 * 1024


def _params(*semantics):
    return pltpu.CompilerParams(dimension_semantics=semantics,
                                vmem_limit_bytes=VMEM_LIMIT_BYTES)


def _split_bf16(a):
    hi = a.astype(BF16)
    lo = (a - hi.astype(F32)).astype(BF16)
    return hi, lo


def _rms_scale(x):
    return x * lax.rsqrt(jnp.mean(x * x, axis=-1, keepdims=True) + EPS)


def _mod_kernel(c_ref, w_ref, b_ref, o_ref):
    c = c_ref[...]
    c_act = c * (1.0 / (1.0 + jnp.exp(-c)))
    o_ref[0] = jnp.dot(c_act, w_ref[0], preferred_element_type=F32) + b_ref[0]


def _modulation(c, w_ada, b_ada):
    depth, d, nd = w_ada.shape
    bsz = c.shape[0]
    rows = -(-bsz // 8) * 8
    c_pad = jnp.zeros((rows, d), F32).at[:bsz].set(c)
    out = pl.pallas_call(
        _mod_kernel,
        grid=(depth, nd // d),
        in_specs=[pl.BlockSpec((rows, d), lambda l, j: (0, 0)),
                  pl.BlockSpec((1, d, d), lambda l, j: (l, 0, j)),
                  pl.BlockSpec((1, 1, d), lambda l, j: (l, 0, j))],
        out_specs=pl.BlockSpec((1, rows, d), lambda l, j: (l, 0, j)),
        out_shape=jax.ShapeDtypeStruct((depth, rows, nd), F32),
        compiler_params=_params("parallel", "parallel"),
        name="adaln_modulation",
    )(c_pad, w_ada, b_ada.reshape(depth, 1, nd))
    return out[:, :bsz].reshape(depth, bsz, N_MOD, d)


def _rope_kernel(pos_ref, freq_ref, sign_ref, cos_ref, sin_ref):
    ang = pos_ref[0] * freq_ref[...]
    cos_ref[0] = jnp.cos(ang)
    sin_ref[0] = jnp.sin(ang) * sign_ref[...]


def _rope_tables(positions):
    bsz, seq = positions.shape
    half = HEAD_DIM // 2
    inv_freq = ROPE_THETA ** (-jnp.arange(0, HEAD_DIM, 2, dtype=F32) / HEAD_DIM)
    freq = jnp.tile(inv_freq, LANES // half).reshape(1, LANES)
    sign = jnp.tile(jnp.concatenate([-jnp.ones((half,), F32), jnp.ones((half,), F32)]),
                    HEADS_PER_TILE).reshape(1, LANES)
    pos = positions.astype(F32).reshape(bsz, seq, 1)
    row = pl.BlockSpec((1, LANES), lambda b: (0, 0))
    tab = pl.BlockSpec((1, seq, LANES), lambda b: (b, 0, 0))
    return pl.pallas_call(
        _rope_kernel,
        grid=(bsz,),
        in_specs=[pl.BlockSpec((1, seq, 1), lambda b: (b, 0, 0)), row, row],
        out_specs=[tab, tab],
        out_shape=[jax.ShapeDtypeStruct((bsz, seq, LANES), F32)] * 2,
        compiler_params=_params("parallel"),
        name="rope_tables",
    )(pos, freq, sign)


def _inproj_kernel(x_ref, mod_ref, g_ref, w_ref, cos_ref, sin_ref, gsgu_ref, wsp_ref, bsp_ref,
                   ya_ref, bq_ref, bk_ref, bv_ref, cq_ref, ck_ref, cv_ref, h_scr, *, tm):
    mod = mod_ref[0]
    h = _rms_scale(x_ref[...]) * g_ref[...] * (1.0 + mod[1:2]) + mod[0:1]
    h_scr[...] = h.astype(BF16)
    scale = HEAD_DIM ** -0.5

    def proj(c0, width):
        return jnp.dot(h_scr[...], w_ref[:, c0:c0 + width], preferred_element_type=F32)

    pa = proj(0, 2 * A_WIDTH)
    u = jax.nn.gelu(pa[:, :A_WIDTH])
    v = jax.nn.gelu(pa[:, A_WIDTH:])
    grp_r = lax.broadcasted_iota(jnp.int32, (A_WIDTH, A_WIDTH), 0) // HEAD_DIM
    grp_c = lax.broadcasted_iota(jnp.int32, (A_WIDTH, A_WIDTH), 1) // HEAD_DIM
    group_ones = jnp.where(grp_r == grp_c, 1.0, 0.0).astype(BF16)
    hi, lo = _split_bf16(v * v)
    ssq = (jnp.dot(hi, group_ones, preferred_element_type=F32)
           + jnp.dot(lo, group_ones, preferred_element_type=F32))
    vn = v * lax.rsqrt(ssq * (1.0 / HEAD_DIM) + EPS) * gsgu_ref[...]
    t_idx = lax.broadcasted_iota(jnp.int32, (CHUNK, A_GROUPS * CHUNK), 0)
    s_idx = lax.broadcasted_iota(jnp.int32, (CHUNK, A_GROUPS * CHUNK), 1) % CHUNK
    w_causal = jnp.where(s_idx <= t_idx, wsp_ref[...], 0.0).astype(BF16)
    lane_grp = lax.broadcasted_iota(jnp.int32, (CHUNK, A_WIDTH), 1) // HEAD_DIM
    for c in range(tm // CHUNK):
        rows = slice(c * CHUNK, (c + 1) * CHUNK)
        vc = vn[rows]
        stacked = jnp.concatenate(
            [jnp.where(lane_grp == g, vc, 0.0) for g in range(A_GROUPS)], axis=0).astype(BF16)
        mixed = jnp.dot(w_causal, stacked, preferred_element_type=F32) + bsp_ref[...]
        ya_ref[rows, :] = (u[rows] * mixed).astype(BF16)

    c0 = 2 * A_WIDTH
    bq_ref[...] = (proj(c0, B_WIDTH) * scale).astype(BF16)
    bk_ref[...] = proj(c0 + B_WIDTH, B_WIDTH).astype(BF16)
    bv_ref[...] = proj(c0 + 2 * B_WIDTH, B_WIDTH).astype(BF16)

    cos = cos_ref[...]
    sin = sin_ref[...]
    first_half = (lax.broadcasted_iota(jnp.int32, (tm, LANES), 1) % HEAD_DIM) < HEAD_DIM // 2

    def rope(pc):
        outs = []
        for p in range(C_WIDTH // LANES):
            xp = pc[:, p * LANES:(p + 1) * LANES]
            rot = jnp.where(first_half,
                            pltpu.roll(xp, LANES - HEAD_DIM // 2, axis=1),
                            pltpu.roll(xp, HEAD_DIM // 2, axis=1))
            outs.append(xp * cos + rot * sin)
        return jnp.concatenate(outs, axis=1)

    c0 += 3 * B_WIDTH
    cq_ref[...] = rope(proj(c0, C_WIDTH)) * scale
    ck_ref[...] = rope(proj(c0 + C_WIDTH, C_WIDTH))
    cv_ref[...] = proj(c0 + 2 * C_WIDTH, C_WIDTH)


def _in_projection(x2d, mod_l, g_mix, w_in, cos, sin, g_sgu, w_sp, b_sp, *, seq, tm):
    tokens, d = x2d.shape
    per_batch = seq // tm
    in_width = w_in.shape[1]
    w_cat = jnp.transpose(w_sp, (1, 0, 2)).reshape(CHUNK, A_GROUPS * CHUNK)
    bias = jnp.repeat(b_sp.T, HEAD_DIM, axis=1)
    tile = lambda w: pl.BlockSpec((tm, w), lambda i: (i, 0))
    full = lambda a: pl.BlockSpec(a.shape, lambda i: (0,) * a.ndim)
    g_mix = g_mix.reshape(1, d)
    g_sgu = g_sgu.reshape(1, A_WIDTH)
    outs = pl.pallas_call(
        functools.partial(_inproj_kernel, tm=tm),
        grid=(tokens // tm,),
        in_specs=[tile(d),
                  pl.BlockSpec((1, N_MOD, d), lambda i: (i // per_batch, 0, 0)),
                  full(g_mix), full(w_in), tile(LANES), tile(LANES),
                  full(g_sgu), full(w_cat), full(bias)],
        out_specs=[tile(A_WIDTH)] + [tile(B_WIDTH)] * 3 + [tile(C_WIDTH)] * 3,
        out_shape=[jax.ShapeDtypeStruct((tokens, w), dt) for w, dt in
                   [(A_WIDTH, BF16)] + [(B_WIDTH, BF16)] * 3 + [(C_WIDTH, F32)] * 3],
        scratch_shapes=[pltpu.VMEM((tm, d), BF16)],
        compiler_params=_params("parallel"),
        name="in_projection",
    )(x2d, mod_l, g_mix, w_in, cos, sin, g_sgu, w_cat, bias)
    assert in_width == 2 * A_WIDTH + 3 * B_WIDTH + 3 * C_WIDTH
    return outs


def _stick_kernel(q_ref, k_ref, v_ref, later_ref, o_ref, acc_scr, carry_scr, *, blk, lead, n_q):
    first_q = pl.program_id(2) * n_q
    rows = HEADS_PER_TILE * blk
    lane_head = lax.broadcasted_iota(jnp.int32, (blk, LANES), 1) // HEAD_DIM
    t_idx = lax.broadcasted_iota(jnp.int32, (rows, blk), 0) % blk
    before = lax.broadcasted_iota(jnp.int32, (rows, blk), 1) < t_idx

    def stacked_q(j):
        q = q_ref[j * blk:(j + 1) * blk, :]
        return jnp.concatenate(
            [jnp.where(lane_head == h, q, jnp.zeros_like(q)) for h in range(HEADS_PER_TILE)], axis=0)

    def visit(q_stack, first_block, n_blocks, diagonal, carry, acc, out):
        start = pl.multiple_of(first_block * blk, blk)
        k_win = k_ref[pl.ds(start, n_blocks * blk), :]
        v_win = v_ref[pl.ds(start, n_blocks * blk), :]
        z = lax.dot_general(q_stack, k_win, (((1,), (1,)), ((), ())), preferred_element_type=F32)
        yield
        log_beta = jnp.minimum(z, 0.0) - jnp.log(1.0 + jnp.exp(-jnp.abs(z)))
        log_stay = log_beta - z
        parts = []
        for w in range(n_blocks):
            stay_w = log_stay[:, w * blk:(w + 1) * blk]
            if diagonal and w == n_blocks - 1:
                stay_w = jnp.where(before, stay_w, 0.0)
            parts.append(jnp.concatenate(_split_bf16(stay_w), axis=1))
        sums = jnp.dot(jnp.concatenate(parts, axis=0), later_ref[...], preferred_element_type=F32)
        yield
        weights = [None] * n_blocks
        for w in reversed(range(n_blocks)):
            sums_w = sums[w * rows:(w + 1) * rows]
            weight = jnp.exp(log_beta[:, w * blk:(w + 1) * blk] + sums_w[:, :blk] + carry)
            if diagonal and w == n_blocks - 1:
                weight = jnp.where(before, weight, 0.0)
            weights[w] = weight.astype(BF16)
            carry = carry + sums_w[:, blk:]
        acc = acc + jnp.dot(jnp.concatenate(weights, axis=1), v_win, preferred_element_type=F32)
        out.append((carry, acc))

    def run_staged(visits):
        while visits:
            visits = [g for g in visits if next(g, True) is None]

    def first_visits(first_q, n_blocks):
        zero = jnp.zeros((rows, blk), F32)
        outs = [[] for _ in range(n_q)]
        run_staged([visit(stacked_q(j), first_q + j - (n_blocks[j] - 1), n_blocks[j], True,
                          zero, zero, outs[j]) for j in range(n_q)])
        for j in range(n_q):
            carry_scr[j], acc_scr[j] = outs[j][0]

    @pl.when(first_q == 0)
    def _():
        first_visits(0, [min(j + 1, lead) for j in range(n_q)])

    @pl.when(first_q > 0)
    def _():
        first_visits(first_q, [lead] * n_q)

    for j in range(n_q):
        def alive(j=j):
            top = jnp.max(carry_scr[j], axis=0, keepdims=True)
            return (top[0, 0] > EXP_ZERO_BELOW).astype(jnp.int32)

        def cond(state):
            kb, go = state
            return jnp.logical_and(kb >= 0, go > 0)

        def body(state, j=j, alive=alive):
            kb, _ = state
            out = []
            run_staged([visit(stacked_q(j), kb, 1, False, carry_scr[j], acc_scr[j], out)])
            carry_scr[j], acc_scr[j] = out[0]
            return kb - 1, alive()

        lax.while_loop(cond, body, (first_q + j - lead, alive()))
        o_ref[j * blk:(j + 1) * blk, :] = jnp.where(
            lane_head == 0, acc_scr[j, :blk], acc_scr[j, blk:]).astype(BF16)


def _stick_breaking(q, k, v, *, bsz, seq):
    tokens, width = q.shape
    blk = Q_BLOCK
    n_q = STICK_Q_BLOCKS_PER_STEP
    assert n_q >= STICK_LEAD_BLOCKS - 1
    steps = seq // (blk * n_q)
    j_idx = lax.broadcasted_iota(jnp.int32, (2 * blk, 2 * blk), 0) % blk
    s_idx = lax.broadcasted_iota(jnp.int32, (2 * blk, 2 * blk), 1)
    later = jnp.where((j_idx > s_idx) | (s_idx >= blk), 1.0, 0.0).astype(BF16)
    q_spec = pl.BlockSpec((n_q * blk, LANES), lambda b, p, i: (b * steps + i, p))
    kv_spec = pl.BlockSpec((seq, LANES), lambda b, p, i: (b, p))
    return pl.pallas_call(
        functools.partial(_stick_kernel, blk=blk, lead=STICK_LEAD_BLOCKS, n_q=n_q),
        grid=(bsz, width // LANES, steps),
        in_specs=[q_spec, kv_spec, kv_spec, pl.BlockSpec(later.shape, lambda b, p, i: (0, 0))],
        out_specs=q_spec,
        out_shape=jax.ShapeDtypeStruct((tokens, width), BF16),
        scratch_shapes=[pltpu.VMEM((n_q, HEADS_PER_TILE * blk, LANES), F32),
                        pltpu.VMEM((n_q, HEADS_PER_TILE * blk, blk), F32)],
        compiler_params=_params("parallel", "parallel", "parallel"),
        name="stick_breaking",
    )(q, k, v, later)


def _window_kernel(q_ref, k_ref, v_ref, o_ref, lse_ref, *, seq, tq):
    pair = pl.program_id(1)
    rows = HEADS_PER_TILE * WINDOW
    lane_head = lax.broadcasted_iota(jnp.int32, (WINDOW, LANES), 1) // HEAD_DIM
    off = (lax.broadcasted_iota(jnp.int32, (rows, 2 * WINDOW), 1)
           - lax.broadcasted_iota(jnp.int32, (rows, 2 * WINDOW), 0) % WINDOW)
    band = (off >= 0) & (off <= WINDOW)
    from_prev = lax.broadcasted_iota(jnp.int32, (rows, 2 * WINDOW), 1) < WINDOW

    def class_rows(first, d):
        return pl.ds(first, WINDOW) if d == 1 else pl.ds(first, WINDOW, stride=d)

    def block(d, r, j, prev, out):
        cur = class_rows(r + d * j, d)
        q = q_ref[cur, :].astype(BF16)
        k_cur = k_ref[cur, :].astype(BF16)
        v_cur = v_ref[cur, :].astype(BF16)
        out.append((k_cur, v_cur))
        if prev is None:
            before = class_rows(r + d * jnp.maximum(j - WINDOW, 0), d)
            prev = (k_ref[before, :].astype(BF16), v_ref[before, :].astype(BF16))
            valid = band & (jnp.logical_not(from_prev) | (j > 0))
        else:
            valid = band
        q_stack = jnp.concatenate(
            [jnp.where(lane_head == h, q, jnp.zeros_like(q)) for h in range(HEADS_PER_TILE)], axis=0)
        z = lax.dot_general(q_stack, jnp.concatenate([prev[0], k_cur], axis=0),
                            (((1,), (1,)), ((), ())), preferred_element_type=F32)
        yield
        z = jnp.where(valid, z, -jnp.inf)
        z_max = jnp.max(z, axis=-1, keepdims=True)
        prob = jnp.exp(z - z_max)
        denom = jnp.sum(prob, axis=-1, keepdims=True)
        pv = jnp.dot(prob.astype(BF16), jnp.concatenate([prev[1], v_cur], axis=0),
                     preferred_element_type=F32)
        yield
        o = pv * (1.0 / denom)
        lse = z_max + jnp.log(denom)
        o_ref[cur, :] = jnp.where(lane_head == 0, o[:WINDOW], o[WINDOW:])
        lse_ref[cur, :] = jnp.where(lane_head == 0, lse[:WINDOW], lse[WINDOW:])

    def tiles(d):
        per_class = seq // d // tq

        def tile(i, _):
            r = i // per_class
            j0 = (i % per_class) * tq
            visits, prev = [], None
            for s in range(tq // WINDOW):
                out = []
                visits.append(block(d, r, j0 + s * WINDOW, prev, out))
                next(visits[-1])
                prev = out[0]
            while visits:
                visits = [g for g in visits if next(g, True) is None]
            return 0

        lax.fori_loop(0, seq // tq, tile, 0)

    for idx, d in enumerate(DILATIONS):
        pl.when(pair == idx)(functools.partial(tiles, d))


def _dilated_window(q, k, v, *, bsz, seq):
    tokens, width = q.shape
    spec = pl.BlockSpec((seq, LANES), lambda b, p: (b, p))
    return pl.pallas_call(
        functools.partial(_window_kernel, seq=seq, tq=seq // max(DILATIONS)),
        grid=(bsz, width // LANES),
        in_specs=[spec] * 3,
        out_specs=[spec] * 2,
        out_shape=[jax.ShapeDtypeStruct((tokens, width), F32)] * 2,
        compiler_params=_params("parallel", "parallel"),
        name="dilated_window",
    )(q, k, v)


def _outproj_kernel(x_ref, mod_ref, ya_ref, yb_ref, oc_ref, lse_ref, w_ref, o_ref, y_scr):
    y_scr[:, :A_WIDTH] = ya_ref[...]
    y_scr[:, A_WIDTH:A_WIDTH + B_WIDTH] = yb_ref[...]
    pairs = C_WIDTH // LANES
    lse = [lse_ref[:, p * LANES:(p + 1) * LANES] for p in range(pairs)]
    top = functools.reduce(jnp.maximum, lse)
    share = [jnp.exp(l - top) for l in lse]
    inv = 1.0 / functools.reduce(lambda a, b: a + b, share)
    c0 = A_WIDTH + B_WIDTH
    for p in range(pairs):
        o_p = oc_ref[:, p * LANES:(p + 1) * LANES]
        y_scr[:, c0 + p * LANES:c0 + (p + 1) * LANES] = (o_p * (share[p] * inv)).astype(BF16)
    y = jnp.dot(y_scr[...], w_ref[...], preferred_element_type=F32)
    o_ref[...] = x_ref[...] + (1.0 + mod_ref[0][2:3]) * y


def _out_projection(x2d, mod_l, y_a, y_b, o_c, lse, w_out, *, seq, tm):
    tokens, d = x2d.shape
    per_batch = seq // tm
    tile = lambda w: pl.BlockSpec((tm, w), lambda i: (i, 0))
    return pl.pallas_call(
        _outproj_kernel,
        grid=(tokens // tm,),
        in_specs=[tile(d),
                  pl.BlockSpec((1, N_MOD, d), lambda i: (i // per_batch, 0, 0)),
                  tile(A_WIDTH), tile(B_WIDTH), tile(C_WIDTH), tile(C_WIDTH),
                  pl.BlockSpec(w_out.shape, lambda i: (0, 0))],
        out_specs=tile(d),
        out_shape=jax.ShapeDtypeStruct((tokens, d), F32),
        scratch_shapes=[pltpu.VMEM((tm, w_out.shape[0]), BF16)],
        compiler_params=_params("parallel"),
        name="out_projection",
    )(x2d, mod_l, y_a, y_b, o_c, lse, w_out)


def _ffn_kernel(x_ref, halo_ref, mod_ref, g_ref, wup_ref, cw_ref, cb_ref, wdn_ref, gfin_ref,
                o_ref, h_scr, acc_scr, *, tm, tn, per_batch, final):
    i = pl.program_id(0)
    mod = mod_ref[0]
    d_ff = wdn_ref.shape[0]

    def norm_mod(xv):
        return _rms_scale(xv) * g_ref[...] * (1.0 + mod[4:5]) + mod[3:4]

    halo = jnp.where(i % per_batch == 0, 0.0, norm_mod(halo_ref[...]))
    h_scr[:CONV_HALO, :] = halo.astype(BF16)
    h_scr[CONV_HALO:, :] = norm_mod(x_ref[...]).astype(BF16)
    acc_scr[...] = jnp.zeros_like(acc_scr)

    def conv(c0):
        up = jnp.dot(h_scr[...], wup_ref[:, c0:c0 + tn], preferred_element_type=F32)
        w = cw_ref[:, c0:c0 + tn]
        out = cb_ref[:, c0:c0 + tn]
        for tap in range(CONV_WIDTH):
            shift = CONV_HALO - (CONV_WIDTH - 1) + tap
            out = out + w[tap:tap + 1] * up[shift:shift + tm]
        return out

    for n in range(d_ff // tn):
        gate = conv(n * tn)
        val = conv(d_ff + n * tn)
        act = gate * (1.0 / (1.0 + jnp.exp(-gate))) * val
        acc_scr[...] += jnp.dot(act.astype(BF16), wdn_ref[n * tn:(n + 1) * tn, :],
                                preferred_element_type=F32)
    x_new = x_ref[...] + (1.0 + mod[5:6]) * acc_scr[...]
    if final:
        x_new = _rms_scale(x_new) * gfin_ref[...]
    o_ref[...] = x_new


def _channel_mixer(x2d, mod_l, g_ffn, w_up, conv_w, conv_b, w_down, g_final, *, seq, tm, tn, final):
    tokens, d = x2d.shape
    per_batch = seq // tm
    halo_blocks = tm // CONV_HALO
    full = lambda a: pl.BlockSpec(a.shape, lambda i: (0,) * a.ndim)
    resident = lambda a: pl.BlockSpec(a.shape, lambda i: (0,) * a.ndim,
                                      pipeline_mode=pl.Buffered(1))
    g_ffn = g_ffn.reshape(1, d)
    conv_b = conv_b.reshape(1, -1)
    g_final = g_final.reshape(1, d)
    return pl.pallas_call(
        functools.partial(_ffn_kernel, tm=tm, tn=tn, per_batch=per_batch, final=final),
        grid=(tokens // tm,),
        in_specs=[pl.BlockSpec((tm, d), lambda i: (i, 0)),
                  pl.BlockSpec((CONV_HALO, d), lambda i: (jnp.maximum(i * halo_blocks - 1, 0), 0)),
                  pl.BlockSpec((1, N_MOD, d), lambda i: (i // per_batch, 0, 0)),
                  full(g_ffn), resident(w_up), full(conv_w), full(conv_b), resident(w_down),
                  full(g_final)],
        out_specs=pl.BlockSpec((tm, d), lambda i: (i, 0)),
        out_shape=jax.ShapeDtypeStruct((tokens, d), F32),
        scratch_shapes=[pltpu.VMEM((tm + CONV_HALO, d), BF16), pltpu.VMEM((tm, d), F32)],
        compiler_params=_params("parallel"),
        name="channel_mixer",
    )(x2d, x2d, mod_l, g_ffn, w_up, conv_w, conv_b, w_down, g_final)


def kernel(x, c, positions, w_ada, b_ada, g_mix, w_in, g_sgu, w_sp, b_sp, w_out, g_ffn, w_up,
           conv_w, conv_b, w_down, g_final):
    bsz, seq, d = x.shape
    depth = w_ada.shape[0]
    tm = min(512, seq)
    assert seq % tm == 0 and seq % (max(DILATIONS) * WINDOW) == 0
    mod = _modulation(c, w_ada, b_ada)
    cos, sin = _rope_tables(positions)
    cos = cos.reshape(bsz * seq, LANES)
    sin = sin.reshape(bsz * seq, LANES)
    x2d = x.reshape(bsz * seq, d)
    for l in range(depth):
        y_a, b_q, b_k, b_v, c_q, c_k, c_v = _in_projection(
            x2d, mod[l], g_mix[l], w_in[l].astype(BF16), cos, sin, g_sgu[l], w_sp[l], b_sp[l],
            seq=seq, tm=tm)
        y_b = _stick_breaking(b_q, b_k, b_v, bsz=bsz, seq=seq)
        o_c, lse = _dilated_window(c_q, c_k, c_v, bsz=bsz, seq=seq)
        x2d = _out_projection(x2d, mod[l], y_a, y_b, o_c, lse, w_out[l].astype(BF16),
                              seq=seq, tm=tm)
        x2d = _channel_mixer(x2d, mod[l], g_ffn[l], w_up[l].astype(BF16), conv_w[l], conv_b[l],
                             w_down[l].astype(BF16), g_final, seq=seq, tm=tm, tn=256,
                             final=(l == depth - 1))
    return x2d.reshape(bsz, seq, d)
```

```python
import functools

import jax
import jax.numpy as jnp
from jax import lax
from jax.experimental import pallas as pl
from jax.experimental.pallas import tpu as pltpu

F32 = jnp.float32
BF16 = jnp.bfloat16

LANES = 128
MXU_COLUMNS = 256
HEAD_DIM = 64
HEADS_PER_TILE = LANES // HEAD_DIM
A_GROUPS = 4
A_WIDTH = A_GROUPS * HEAD_DIM
CHUNK = 128
B_WIDTH = 6 * HEAD_DIM
C_WIDTH = 6 * HEAD_DIM
DILATIONS = (1, 4, 16)
WINDOW = 128
Q_BLOCK = 128
STICK_LEAD_BLOCKS = 3
STICK_Q_BLOCKS_PER_STEP = 4
CONV_WIDTH = 3
CONV_HALO = 8
ROPE_THETA = 10000.0
EPS = 1e-6
N_MOD = 6
EXP_ZERO_BELOW = -104.0
VMEM_LIMIT_BYTES = 56 * 1024 * 1024


def _params(*semantics):
    return pltpu.CompilerParams(dimension_semantics=semantics,
                                vmem_limit_bytes=VMEM_LIMIT_BYTES)


def _split_bf16(a):
    hi = a.astype(BF16)
    lo = (a - hi.astype(F32)).astype(BF16)
    return hi, lo


def _rms_scale(x):
    return x * lax.rsqrt(jnp.mean(x * x, axis=-1, keepdims=True) + EPS)


def _mod_kernel(c_ref, w_ref, b_ref, o_ref):
    c = c_ref[...]
    c_act = c * (1.0 / (1.0 + jnp.exp(-c)))
    o_ref[0] = jnp.dot(c_act, w_ref[0], preferred_element_type=F32) + b_ref[0]


def _modulation(c, w_ada, b_ada):
    depth, d, nd = w_ada.shape
    bsz = c.shape[0]
    rows = -(-bsz // 8) * 8
    c_pad = jnp.zeros((rows, d), F32).at[:bsz].set(c)
    out = pl.pallas_call(
        _mod_kernel,
        grid=(depth, nd // d),
        in_specs=[pl.BlockSpec((rows, d), lambda l, j: (0, 0)),
                  pl.BlockSpec((1, d, d), lambda l, j: (l, 0, j)),
                  pl.BlockSpec((1, 1, d), lambda l, j: (l, 0, j))],
        out_specs=pl.BlockSpec((1, rows, d), lambda l, j: (l, 0, j)),
        out_shape=jax.ShapeDtypeStruct((depth, rows, nd), F32),
        compiler_params=_params("parallel", "parallel"),
        name="adaln_modulation",
    )(c_pad, w_ada, b_ada.reshape(depth, 1, nd))
    return out[:, :bsz].reshape(depth, bsz, N_MOD, d)


def _rope_kernel(pos_ref, freq_ref, sign_ref, cos_ref, sin_ref):
    ang = pos_ref[0] * freq_ref[...]
    cos_ref[0] = jnp.cos(ang)
    sin_ref[0] = jnp.sin(ang) * sign_ref[...]


def _rope_tables(positions):
    bsz, seq = positions.shape
    half = HEAD_DIM // 2
    inv_freq = ROPE_THETA ** (-jnp.arange(0, HEAD_DIM, 2, dtype=F32) / HEAD_DIM)
    freq = jnp.tile(inv_freq, LANES // half).reshape(1, LANES)
    sign = jnp.tile(jnp.concatenate([-jnp.ones((half,), F32), jnp.ones((half,), F32)]),
                    HEADS_PER_TILE).reshape(1, LANES)
    pos = positions.astype(F32).reshape(bsz, seq, 1)
    row = pl.BlockSpec((1, LANES), lambda b: (0, 0))
    tab = pl.BlockSpec((1, seq, LANES), lambda b: (b, 0, 0))
    return pl.pallas_call(
        _rope_kernel,
        grid=(bsz,),
        in_specs=[pl.BlockSpec((1, seq, 1), lambda b: (b, 0, 0)), row, row],
        out_specs=[tab, tab],
        out_shape=[jax.ShapeDtypeStruct((bsz, seq, LANES), F32)] * 2,
        compiler_params=_params("parallel"),
        name="rope_tables",
    )(pos, freq, sign)


def _inproj_kernel(x_ref, mod_ref, g_ref, w_ref, cos_ref, sin_ref, gsgu_ref, wsp_ref, bsp_ref,
                   ya_ref, bq_ref, bk_ref, bv_ref, cq_ref, ck_ref, cv_ref, h_scr, *, tm):
    mod = mod_ref[0]
    h = _rms_scale(x_ref[...]) * g_ref[...] * (1.0 + mod[1:2]) + mod[0:1]
    h_scr[...] = h.astype(BF16)
    scale = HEAD_DIM ** -0.5

    def proj(c0, width):
        return jnp.dot(h_scr[...], w_ref[:, c0:c0 + width], preferred_element_type=F32)

    pa = proj(0, 2 * A_WIDTH)
    u = jax.nn.gelu(pa[:, :A_WIDTH])
    v = jax.nn.gelu(pa[:, A_WIDTH:])
    grp_r = lax.broadcasted_iota(jnp.int32, (A_WIDTH, A_WIDTH), 0) // HEAD_DIM
    grp_c = lax.broadcasted_iota(jnp.int32, (A_WIDTH, A_WIDTH), 1) // HEAD_DIM
    group_ones = jnp.where(grp_r == grp_c, 1.0, 0.0).astype(BF16)
    hi, lo = _split_bf16(v * v)
    ssq = (jnp.dot(hi, group_ones, preferred_element_type=F32)
           + jnp.dot(lo, group_ones, preferred_element_type=F32))
    vn = v * lax.rsqrt(ssq * (1.0 / HEAD_DIM) + EPS) * gsgu_ref[...]
    t_idx = lax.broadcasted_iota(jnp.int32, (CHUNK, A_GROUPS * CHUNK), 0)
    s_idx = lax.broadcasted_iota(jnp.int32, (CHUNK, A_GROUPS * CHUNK), 1) % CHUNK
    w_causal = jnp.where(s_idx <= t_idx, wsp_ref[...], 0.0).astype(BF16)
    lane_grp = lax.broadcasted_iota(jnp.int32, (CHUNK, A_WIDTH), 1) // HEAD_DIM
    for c in range(tm // CHUNK):
        rows = slice(c * CHUNK, (c + 1) * CHUNK)
        vc = vn[rows]
        stacked = jnp.concatenate(
            [jnp.where(lane_grp == g, vc, 0.0) for g in range(A_GROUPS)], axis=0).astype(BF16)
        mixed = jnp.dot(w_causal, stacked, preferred_element_type=F32) + bsp_ref[...]
        ya_ref[rows, :] = (u[rows] * mixed).astype(BF16)

    cos = cos_ref[...]
    sin = sin_ref[...]
    first_half = (lax.broadcasted_iota(jnp.int32, (tm, LANES), 1) % HEAD_DIM) < HEAD_DIM // 2

    def rope(pc):
        outs = []
        for p in range(C_WIDTH // LANES):
            xp = pc[:, p * LANES:(p + 1) * LANES]
            rot = jnp.where(first_half,
                            pltpu.roll(xp, LANES - HEAD_DIM // 2, axis=1),
                            pltpu.roll(xp, HEAD_DIM // 2, axis=1))
            outs.append(xp * cos + rot * sin)
        return jnp.concatenate(outs, axis=1)

    assert B_WIDTH == C_WIDTH and (2 * B_WIDTH) % MXU_COLUMNS == 0
    c0 = 2 * A_WIDTH
    qk = proj(c0, 2 * B_WIDTH)
    bq_ref[...] = (qk[:, :B_WIDTH] * scale).astype(BF16)
    bk_ref[...] = qk[:, B_WIDTH:].astype(BF16)
    vq = proj(c0 + 2 * B_WIDTH, 2 * B_WIDTH)
    bv_ref[...] = vq[:, :B_WIDTH].astype(BF16)
    cq_ref[...] = rope(vq[:, B_WIDTH:]) * scale
    kv = proj(c0 + 4 * B_WIDTH, 2 * B_WIDTH)
    ck_ref[...] = rope(kv[:, :C_WIDTH])
    cv_ref[...] = kv[:, C_WIDTH:]


def _in_projection(x2d, mod_l, g_mix, w_in, cos, sin, g_sgu, w_sp, b_sp, *, seq, tm):
    tokens, d = x2d.shape
    per_batch = seq // tm
    in_width = w_in.shape[1]
    w_cat = jnp.transpose(w_sp, (1, 0, 2)).reshape(CHUNK, A_GROUPS * CHUNK)
    bias = jnp.repeat(b_sp.T, HEAD_DIM, axis=1)
    tile = lambda w: pl.BlockSpec((tm, w), lambda i: (i, 0))
    full = lambda a: pl.BlockSpec(a.shape, lambda i: (0,) * a.ndim)
    g_mix = g_mix.reshape(1, d)
    g_sgu = g_sgu.reshape(1, A_WIDTH)
    outs = pl.pallas_call(
        functools.partial(_inproj_kernel, tm=tm),
        grid=(tokens // tm,),
        in_specs=[tile(d),
                  pl.BlockSpec((1, N_MOD, d), lambda i: (i // per_batch, 0, 0)),
                  full(g_mix), full(w_in), tile(LANES), tile(LANES),
                  full(g_sgu), full(w_cat), full(bias)],
        out_specs=[tile(A_WIDTH)] + [tile(B_WIDTH)] * 3 + [tile(C_WIDTH)] * 3,
        out_shape=[jax.ShapeDtypeStruct((tokens, w), dt) for w, dt in
                   [(A_WIDTH, BF16)] + [(B_WIDTH, BF16)] * 3 + [(C_WIDTH, F32)] * 3],
        scratch_shapes=[pltpu.VMEM((tm, d), BF16)],
        compiler_params=_params("parallel"),
        name="in_projection",
    )(x2d, mod_l, g_mix, w_in, cos, sin, g_sgu, w_cat, bias)
    assert in_width == 2 * A_WIDTH + 3 * B_WIDTH + 3 * C_WIDTH
    return outs


def _stick_kernel(q_ref, k_ref, v_ref, later_ref, o_ref, acc_scr, carry_scr, *, blk, lead, n_q):
    first_q = pl.program_id(2) * n_q
    rows = HEADS_PER_TILE * blk
    lane_head = lax.broadcasted_iota(jnp.int32, (blk, LANES), 1) // HEAD_DIM
    t_idx = lax.broadcasted_iota(jnp.int32, (rows, blk), 0) % blk
    before = lax.broadcasted_iota(jnp.int32, (rows, blk), 1) < t_idx

    def stacked_q(j):
        q = q_ref[j * blk:(j + 1) * blk, :]
        return jnp.concatenate(
            [jnp.where(lane_head == h, q, jnp.zeros_like(q)) for h in range(HEADS_PER_TILE)], axis=0)

    def visit(q_stack, first_block, n_blocks, diagonal, carry, acc, out):
        start = pl.multiple_of(first_block * blk, blk)
        k_win = k_ref[pl.ds(start, n_blocks * blk), :]
        v_win = v_ref[pl.ds(start, n_blocks * blk), :]
        z = lax.dot_general(q_stack, k_win, (((1,), (1,)), ((), ())), preferred_element_type=F32)
        yield
        log_beta = jnp.minimum(z, 0.0) - jnp.log(1.0 + jnp.exp(-jnp.abs(z)))
        log_stay = log_beta - z
        parts = []
        for w in range(n_blocks):
            stay_w = log_stay[:, w * blk:(w + 1) * blk]
            if diagonal and w == n_blocks - 1:
                stay_w = jnp.where(before, stay_w, 0.0)
            parts.append(jnp.concatenate(_split_bf16(stay_w), axis=1))
        sums = jnp.dot(jnp.concatenate(parts, axis=0), later_ref[...], preferred_element_type=F32)
        yield
        weights = [None] * n_blocks
        for w in reversed(range(n_blocks)):
            sums_w = sums[w * rows:(w + 1) * rows]
            weight = jnp.exp(log_beta[:, w * blk:(w + 1) * blk] + sums_w[:, :blk] + carry)
            if diagonal and w == n_blocks - 1:
                weight = jnp.where(before, weight, 0.0)
            weights[w] = weight.astype(BF16)
            carry = carry + sums_w[:, blk:]
        acc = acc + jnp.dot(jnp.concatenate(weights, axis=1), v_win, preferred_element_type=F32)
        out.append((carry, acc))

    def run_staged(visits):
        while visits:
            visits = [g for g in visits if next(g, True) is None]

    def first_visits(first_q, n_blocks):
        zero = jnp.zeros((rows, blk), F32)
        outs = [[] for _ in range(n_q)]
        run_staged([visit(stacked_q(j), first_q + j - (n_blocks[j] - 1), n_blocks[j], True,
                          zero, zero, outs[j]) for j in range(n_q)])
        for j in range(n_q):
            carry_scr[j], acc_scr[j] = outs[j][0]

    @pl.when(first_q == 0)
    def _():
        first_visits(0, [min(j + 1, lead) for j in range(n_q)])

    @pl.when(first_q > 0)
    def _():
        first_visits(first_q, [lead] * n_q)

    for j in range(n_q):
        def alive(j=j):
            top = jnp.max(carry_scr[j], axis=0, keepdims=True)
            return (top[0, 0] > EXP_ZERO_BELOW).astype(jnp.int32)

        def cond(state):
            kb, go = state
            return jnp.logical_and(kb >= 0, go > 0)

        def body(state, j=j, alive=alive):
            kb, _ = state
            out = []
            run_staged([visit(stacked_q(j), kb, 1, False, carry_scr[j], acc_scr[j], out)])
            carry_scr[j], acc_scr[j] = out[0]
            return kb - 1, alive()

        lax.while_loop(cond, body, (first_q + j - lead, alive()))
        o_ref[j * blk:(j + 1) * blk, :] = jnp.where(
            lane_head == 0, acc_scr[j, :blk], acc_scr[j, blk:]).astype(BF16)


def _stick_breaking(q, k, v, *, bsz, seq):
    tokens, width = q.shape
    blk = Q_BLOCK
    n_q = STICK_Q_BLOCKS_PER_STEP
    assert n_q >= STICK_LEAD_BLOCKS - 1
    steps = seq // (blk * n_q)
    j_idx = lax.broadcasted_iota(jnp.int32, (2 * blk, 2 * blk), 0) % blk
    s_idx = lax.broadcasted_iota(jnp.int32, (2 * blk, 2 * blk), 1)
    later = jnp.where((j_idx > s_idx) | (s_idx >= blk), 1.0, 0.0).astype(BF16)
    q_spec = pl.BlockSpec((n_q * blk, LANES), lambda b, p, i: (b * steps + i, p))
    kv_spec = pl.BlockSpec((seq, LANES), lambda b, p, i: (b, p))
    return pl.pallas_call(
        functools.partial(_stick_kernel, blk=blk, lead=STICK_LEAD_BLOCKS, n_q=n_q),
        grid=(bsz, width // LANES, steps),
        in_specs=[q_spec, kv_spec, kv_spec, pl.BlockSpec(later.shape, lambda b, p, i: (0, 0))],
        out_specs=q_spec,
        out_shape=jax.ShapeDtypeStruct((tokens, width), BF16),
        scratch_shapes=[pltpu.VMEM((n_q, HEADS_PER_TILE * blk, LANES), F32),
                        pltpu.VMEM((n_q, HEADS_PER_TILE * blk, blk), F32)],
        compiler_params=_params("parallel", "parallel", "parallel"),
        name="stick_breaking",
    )(q, k, v, later)


def _window_kernel(q_ref, k_ref, v_ref, o_ref, lse_ref, *, seq, tq):
    pair = pl.program_id(1)
    rows = HEADS_PER_TILE * WINDOW
    lane_head = lax.broadcasted_iota(jnp.int32, (WINDOW, LANES), 1) // HEAD_DIM
    off = (lax.broadcasted_iota(jnp.int32, (rows, 2 * WINDOW), 1)
           - lax.broadcasted_iota(jnp.int32, (rows, 2 * WINDOW), 0) % WINDOW)
    band = (off >= 0) & (off <= WINDOW)
    from_prev = lax.broadcasted_iota(jnp.int32, (rows, 2 * WINDOW), 1) < WINDOW

    def class_rows(first, d):
        return pl.ds(first, WINDOW) if d == 1 else pl.ds(first, WINDOW, stride=d)

    def block(d, r, j, prev, out):
        cur = class_rows(r + d * j, d)
        q = q_ref[cur, :].astype(BF16)
        k_cur = k_ref[cur, :].astype(BF16)
        v_cur = v_ref[cur, :].astype(BF16)
        out.append((k_cur, v_cur))
        if prev is None:
            before = class_rows(r + d * jnp.maximum(j - WINDOW, 0), d)
            prev = (k_ref[before, :].astype(BF16), v_ref[before, :].astype(BF16))
            valid = band & (jnp.logical_not(from_prev) | (j > 0))
        else:
            valid = band
        q_stack = jnp.concatenate(
            [jnp.where(lane_head == h, q, jnp.zeros_like(q)) for h in range(HEADS_PER_TILE)], axis=0)
        z = lax.dot_general(q_stack, jnp.concatenate([prev[0], k_cur], axis=0),
                            (((1,), (1,)), ((), ())), preferred_element_type=F32)
        yield
        z = jnp.where(valid, z, -jnp.inf)
        z_max = jnp.max(z, axis=-1, keepdims=True)
        prob = jnp.exp(z - z_max)
        denom = jnp.sum(prob, axis=-1, keepdims=True)
        pv = jnp.dot(prob.astype(BF16), jnp.concatenate([prev[1], v_cur], axis=0),
                     preferred_element_type=F32)
        yield
        o = pv * (1.0 / denom)
        lse = z_max + jnp.log(denom)
        o_ref[cur, :] = jnp.where(lane_head == 0, o[:WINDOW], o[WINDOW:])
        lse_ref[cur, :] = jnp.where(lane_head == 0, lse[:WINDOW], lse[WINDOW:])

    def tiles(d):
        per_class = seq // d // tq

        def tile(i, _):
            r = i // per_class
            j0 = (i % per_class) * tq
            visits, prev = [], None
            for s in range(tq // WINDOW):
                out = []
                visits.append(block(d, r, j0 + s * WINDOW, prev, out))
                next(visits[-1])
                prev = out[0]
            while visits:
                visits = [g for g in visits if next(g, True) is None]
            return 0

        lax.fori_loop(0, seq // tq, tile, 0)

    for idx, d in enumerate(DILATIONS):
        pl.when(pair == idx)(functools.partial(tiles, d))


def _dilated_window(q, k, v, *, bsz, seq):
    tokens, width = q.shape
    spec = pl.BlockSpec((seq, LANES), lambda b, p: (b, p))
    return pl.pallas_call(
        functools.partial(_window_kernel, seq=seq, tq=seq // max(DILATIONS)),
        grid=(bsz, width // LANES),
        in_specs=[spec] * 3,
        out_specs=[spec] * 2,
        out_shape=[jax.ShapeDtypeStruct((tokens, width), F32)] * 2,
        compiler_params=_params("parallel", "parallel"),
        name="dilated_window",
    )(q, k, v)


def _outproj_kernel(x_ref, mod_ref, ya_ref, yb_ref, oc_ref, lse_ref, w_ref, o_ref, y_scr):
    y_scr[:, :A_WIDTH] = ya_ref[...]
    y_scr[:, A_WIDTH:A_WIDTH + B_WIDTH] = yb_ref[...]
    pairs = C_WIDTH // LANES
    lse = [lse_ref[:, p * LANES:(p + 1) * LANES] for p in range(pairs)]
    top = functools.reduce(jnp.maximum, lse)
    share = [jnp.exp(l - top) for l in lse]
    inv = 1.0 / functools.reduce(lambda a, b: a + b, share)
    c0 = A_WIDTH + B_WIDTH
    for p in range(pairs):
        o_p = oc_ref[:, p * LANES:(p + 1) * LANES]
        y_scr[:, c0 + p * LANES:c0 + (p + 1) * LANES] = (o_p * (share[p] * inv)).astype(BF16)
    y = jnp.dot(y_scr[...], w_ref[...], preferred_element_type=F32)
    o_ref[...] = x_ref[...] + (1.0 + mod_ref[0][2:3]) * y


def _out_projection(x2d, mod_l, y_a, y_b, o_c, lse, w_out, *, seq, tm):
    tokens, d = x2d.shape
    per_batch = seq // tm
    tile = lambda w: pl.BlockSpec((tm, w), lambda i: (i, 0))
    return pl.pallas_call(
        _outproj_kernel,
        grid=(tokens // tm,),
        in_specs=[tile(d),
                  pl.BlockSpec((1, N_MOD, d), lambda i: (i // per_batch, 0, 0)),
                  tile(A_WIDTH), tile(B_WIDTH), tile(C_WIDTH), tile(C_WIDTH),
                  pl.BlockSpec(w_out.shape, lambda i: (0, 0))],
        out_specs=tile(d),
        out_shape=jax.ShapeDtypeStruct((tokens, d), F32),
        scratch_shapes=[pltpu.VMEM((tm, w_out.shape[0]), BF16)],
        compiler_params=_params("parallel"),
        name="out_projection",
    )(x2d, mod_l, y_a, y_b, o_c, lse, w_out)


def _ffn_kernel(x_ref, halo_ref, mod_ref, g_ref, wup_ref, cw_ref, cb_ref, wdn_ref, gfin_ref,
                o_ref, h_scr, up_scr, act_scr, *, tm, tn, per_batch, final):
    i = pl.program_id(0)
    mod = mod_ref[0]
    d_ff = wdn_ref.shape[0]

    def norm_mod(xv):
        return _rms_scale(xv) * g_ref[...] * (1.0 + mod[4:5]) + mod[3:4]

    halo = jnp.where(i % per_batch == 0, 0.0, norm_mod(halo_ref[...]))
    h_scr[:CONV_HALO, :] = halo.astype(BF16)
    h_scr[CONV_HALO:, :] = norm_mod(x_ref[...]).astype(BF16)

    def conv(slot, c0):
        up_scr[slot] = jnp.dot(h_scr[...], wup_ref[:, c0:c0 + tn], preferred_element_type=F32)
        w = cw_ref[:, c0:c0 + tn]
        out = cb_ref[:, c0:c0 + tn]
        for tap in range(CONV_WIDTH):
            shift = CONV_HALO - (CONV_WIDTH - 1) + tap
            out = out + w[tap:tap + 1] * up_scr[slot, shift:shift + tm, :]
        return out

    for n in range(d_ff // tn):
        slot = 2 * (n % (up_scr.shape[0] // 2))
        gate = conv(slot, n * tn)
        val = conv(slot + 1, d_ff + n * tn)
        act = gate * (1.0 / (1.0 + jnp.exp(-gate))) * val
        act_scr[:, n * tn:(n + 1) * tn] = act.astype(BF16)
    down = jnp.dot(act_scr[...], wdn_ref[...], preferred_element_type=F32)
    x_new = x_ref[...] + (1.0 + mod[5:6]) * down
    if final:
        x_new = _rms_scale(x_new) * gfin_ref[...]
    o_ref[...] = x_new


def _channel_mixer(x2d, mod_l, g_ffn, w_up, conv_w, conv_b, w_down, g_final, *, seq, tm, tn, final):
    tokens, d = x2d.shape
    per_batch = seq // tm
    halo_blocks = tm // CONV_HALO
    full = lambda a: pl.BlockSpec(a.shape, lambda i: (0,) * a.ndim)
    resident = lambda a: pl.BlockSpec(a.shape, lambda i: (0,) * a.ndim,
                                      pipeline_mode=pl.Buffered(1))
    g_ffn = g_ffn.reshape(1, d)
    conv_b = conv_b.reshape(1, -1)
    g_final = g_final.reshape(1, d)
    return pl.pallas_call(
        functools.partial(_ffn_kernel, tm=tm, tn=tn, per_batch=per_batch, final=final),
        grid=(tokens // tm,),
        in_specs=[pl.BlockSpec((tm, d), lambda i: (i, 0)),
                  pl.BlockSpec((CONV_HALO, d), lambda i: (jnp.maximum(i * halo_blocks - 1, 0), 0)),
                  pl.BlockSpec((1, N_MOD, d), lambda i: (i // per_batch, 0, 0)),
                  full(g_ffn), resident(w_up), full(conv_w), full(conv_b), resident(w_down),
                  full(g_final)],
        out_specs=pl.BlockSpec((tm, d), lambda i: (i, 0)),
        out_shape=jax.ShapeDtypeStruct((tokens, d), F32),
        scratch_shapes=[pltpu.VMEM((tm + CONV_HALO, d), BF16),
                        pltpu.VMEM((4, tm + CONV_HALO, tn), F32),
                        pltpu.VMEM((tm, w_down.shape[0]), BF16)],
        compiler_params=_params("parallel"),
        name="channel_mixer",
    )(x2d, x2d, mod_l, g_ffn, w_up, conv_w, conv_b, w_down, g_final)


def kernel(x, c, positions, w_ada, b_ada, g_mix, w_in, g_sgu, w_sp, b_sp, w_out, g_ffn, w_up,
           conv_w, conv_b, w_down, g_final):
    bsz, seq, d = x.shape
    depth = w_ada.shape[0]
    tm = min(512, seq)
    assert seq % tm == 0 and seq % (max(DILATIONS) * WINDOW) == 0
    mod = _modulation(c, w_ada, b_ada)
    cos, sin = _rope_tables(positions)
    cos = cos.reshape(bsz * seq, LANES)
    sin = sin.reshape(bsz * seq, LANES)
    x2d = x.reshape(bsz * seq, d)
    for l in range(depth):
        y_a, b_q, b_k, b_v, c_q, c_k, c_v = _in_projection(
            x2d, mod[l], g_mix[l], w_in[l].astype(BF16), cos, sin, g_sgu[l], w_sp[l], b_sp[l],
            seq=seq, tm=tm)
        y_b = _stick_breaking(b_q, b_k, b_v, bsz=bsz, seq=seq)
        o_c, lse = _dilated_window(c_q, c_k, c_v, bsz=bsz, seq=seq)
        x2d = _out_projection(x2d, mod[l], y_a, y_b, o_c, lse, w_out[l].astype(BF16),
                              seq=seq, tm=tm)
        x2d = _channel_mixer(x2d, mod[l], g_ffn[l], w_up[l].astype(BF16), conv_w[l], conv_b[l],
                             w_down[l].astype(BF16), g_final, seq=seq, tm=tm, tn=256,
                             final=(l == depth - 1))
    return x2d.reshape(bsz, seq, d)
```

```python
import functools

import jax
import jax.numpy as jnp
from jax import lax
from jax.experimental import pallas as pl
from jax.experimental.pallas import tpu as pltpu

F32 = jnp.float32
BF16 = jnp.bfloat16

LANES = 128
MXU_COLUMNS = 256
HEAD_DIM = 64
HEADS_PER_TILE = LANES // HEAD_DIM
A_GROUPS = 4
A_WIDTH = A_GROUPS * HEAD_DIM
CHUNK = 128
B_WIDTH = 6 * HEAD_DIM
C_WIDTH = 6 * HEAD_DIM
DILATIONS = (1, 4, 16)
WINDOW = 128
Q_BLOCK = 128
STICK_LEAD_BLOCKS = 3
STICK_Q_BLOCKS_PER_STEP = 8
CONV_WIDTH = 3
CONV_HALO = 8
ROPE_THETA = 10000.0
EPS = 1e-6
N_MOD = 6
EXP_ZERO_BELOW = -104.0
VMEM_LIMIT_BYTES = 56 * 1024 * 1024


def _params(*semantics):
    return pltpu.CompilerParams(dimension_semantics=semantics,
                                vmem_limit_bytes=VMEM_LIMIT_BYTES)


def _split_bf16(a):
    hi = a.astype(BF16)
    lo = (a - hi.astype(F32)).astype(BF16)
    return hi, lo


def _rms_scale(x):
    return x * lax.rsqrt(jnp.mean(x * x, axis=-1, keepdims=True) + EPS)


def _mod_kernel(c_ref, w_ref, b_ref, o_ref):
    c = c_ref[...]
    c_act = c * (1.0 / (1.0 + jnp.exp(-c)))
    o_ref[0] = jnp.dot(c_act, w_ref[0], preferred_element_type=F32) + b_ref[0]


def _modulation(c, w_ada, b_ada):
    depth, d, nd = w_ada.shape
    bsz = c.shape[0]
    rows = -(-bsz // 8) * 8
    c_pad = jnp.zeros((rows, d), F32).at[:bsz].set(c)
    out = pl.pallas_call(
        _mod_kernel,
        grid=(depth, nd // d),
        in_specs=[pl.BlockSpec((rows, d), lambda l, j: (0, 0)),
                  pl.BlockSpec((1, d, d), lambda l, j: (l, 0, j)),
                  pl.BlockSpec((1, 1, d), lambda l, j: (l, 0, j))],
        out_specs=pl.BlockSpec((1, rows, d), lambda l, j: (l, 0, j)),
        out_shape=jax.ShapeDtypeStruct((depth, rows, nd), F32),
        compiler_params=_params("parallel", "parallel"),
        name="adaln_modulation",
    )(c_pad, w_ada, b_ada.reshape(depth, 1, nd))
    return out[:, :bsz].reshape(depth, bsz, N_MOD, d)


def _rope_kernel(pos_ref, freq_ref, sign_ref, cos_ref, sin_ref):
    ang = pos_ref[0] * freq_ref[...]
    cos_ref[0] = jnp.cos(ang)
    sin_ref[0] = jnp.sin(ang) * sign_ref[...]


def _rope_tables(positions):
    bsz, seq = positions.shape
    half = HEAD_DIM // 2
    inv_freq = ROPE_THETA ** (-jnp.arange(0, HEAD_DIM, 2, dtype=F32) / HEAD_DIM)
    freq = jnp.tile(inv_freq, LANES // half).reshape(1, LANES)
    sign = jnp.tile(jnp.concatenate([-jnp.ones((half,), F32), jnp.ones((half,), F32)]),
                    HEADS_PER_TILE).reshape(1, LANES)
    pos = positions.astype(F32).reshape(bsz, seq, 1)
    row = pl.BlockSpec((1, LANES), lambda b: (0, 0))
    tab = pl.BlockSpec((1, seq, LANES), lambda b: (b, 0, 0))
    return pl.pallas_call(
        _rope_kernel,
        grid=(bsz,),
        in_specs=[pl.BlockSpec((1, seq, 1), lambda b: (b, 0, 0)), row, row],
        out_specs=[tab, tab],
        out_shape=[jax.ShapeDtypeStruct((bsz, seq, LANES), F32)] * 2,
        compiler_params=_params("parallel"),
        name="rope_tables",
    )(pos, freq, sign)


def _inproj_kernel(x_ref, mod_ref, g_ref, w_ref, cos_ref, sin_ref, gsgu_ref, wsp_ref, bsp_ref,
                   ya_ref, bq_ref, bk_ref, bv_ref, cq_ref, ck_ref, cv_ref, h_scr, *, tm):
    mod = mod_ref[0]
    h = _rms_scale(x_ref[...]) * g_ref[...] * (1.0 + mod[1:2]) + mod[0:1]
    h_scr[...] = h.astype(BF16)
    scale = HEAD_DIM ** -0.5

    def proj(c0, width):
        return jnp.dot(h_scr[...], w_ref[:, c0:c0 + width], preferred_element_type=F32)

    pa = proj(0, 2 * A_WIDTH)
    u = jax.nn.gelu(pa[:, :A_WIDTH])
    v = jax.nn.gelu(pa[:, A_WIDTH:])
    grp_r = lax.broadcasted_iota(jnp.int32, (A_WIDTH, A_WIDTH), 0) // HEAD_DIM
    grp_c = lax.broadcasted_iota(jnp.int32, (A_WIDTH, A_WIDTH), 1) // HEAD_DIM
    group_ones = jnp.where(grp_r == grp_c, 1.0, 0.0).astype(BF16)
    hi, lo = _split_bf16(v * v)
    ssq = (jnp.dot(hi, group_ones, preferred_element_type=F32)
           + jnp.dot(lo, group_ones, preferred_element_type=F32))
    vn = v * lax.rsqrt(ssq * (1.0 / HEAD_DIM) + EPS) * gsgu_ref[...]
    t_idx = lax.broadcasted_iota(jnp.int32, (CHUNK, A_GROUPS * CHUNK), 0)
    s_idx = lax.broadcasted_iota(jnp.int32, (CHUNK, A_GROUPS * CHUNK), 1) % CHUNK
    w_causal = jnp.where(s_idx <= t_idx, wsp_ref[...], 0.0).astype(BF16)
    lane_grp = lax.broadcasted_iota(jnp.int32, (CHUNK, A_WIDTH), 1) // HEAD_DIM
    for c in range(tm // CHUNK):
        rows = slice(c * CHUNK, (c + 1) * CHUNK)
        vc = vn[rows]
        stacked = jnp.concatenate(
            [jnp.where(lane_grp == g, vc, 0.0) for g in range(A_GROUPS)], axis=0).astype(BF16)
        mixed = jnp.dot(w_causal, stacked, preferred_element_type=F32) + bsp_ref[...]
        ya_ref[rows, :] = (u[rows] * mixed).astype(BF16)

    cos = cos_ref[...]
    sin = sin_ref[...]
    first_half = (lax.broadcasted_iota(jnp.int32, (tm, LANES), 1) % HEAD_DIM) < HEAD_DIM // 2

    def rope(pc):
        outs = []
        for p in range(C_WIDTH // LANES):
            xp = pc[:, p * LANES:(p + 1) * LANES]
            rot = jnp.where(first_half,
                            pltpu.roll(xp, LANES - HEAD_DIM // 2, axis=1),
                            pltpu.roll(xp, HEAD_DIM // 2, axis=1))
            outs.append(xp * cos + rot * sin)
        return jnp.concatenate(outs, axis=1)

    assert B_WIDTH == C_WIDTH and (2 * B_WIDTH) % MXU_COLUMNS == 0
    c0 = 2 * A_WIDTH
    qk = proj(c0, 2 * B_WIDTH)
    bq_ref[...] = (qk[:, :B_WIDTH] * scale).astype(BF16)
    bk_ref[...] = qk[:, B_WIDTH:].astype(BF16)
    vq = proj(c0 + 2 * B_WIDTH, 2 * B_WIDTH)
    bv_ref[...] = vq[:, :B_WIDTH].astype(BF16)
    cq_ref[...] = rope(vq[:, B_WIDTH:]) * scale
    kv = proj(c0 + 4 * B_WIDTH, 2 * B_WIDTH)
    ck_ref[...] = rope(kv[:, :C_WIDTH])
    cv_ref[...] = kv[:, C_WIDTH:]


def _in_projection(x2d, mod_l, g_mix, w_in, cos, sin, g_sgu, w_sp, b_sp, *, seq, tm):
    tokens, d = x2d.shape
    per_batch = seq // tm
    in_width = w_in.shape[1]
    w_cat = jnp.transpose(w_sp, (1, 0, 2)).reshape(CHUNK, A_GROUPS * CHUNK)
    bias = jnp.repeat(b_sp.T, HEAD_DIM, axis=1)
    tile = lambda w: pl.BlockSpec((tm, w), lambda i: (i, 0))
    full = lambda a: pl.BlockSpec(a.shape, lambda i: (0,) * a.ndim)
    g_mix = g_mix.reshape(1, d)
    g_sgu = g_sgu.reshape(1, A_WIDTH)
    outs = pl.pallas_call(
        functools.partial(_inproj_kernel, tm=tm),
        grid=(tokens // tm,),
        in_specs=[tile(d),
                  pl.BlockSpec((1, N_MOD, d), lambda i: (i // per_batch, 0, 0)),
                  full(g_mix), full(w_in), tile(LANES), tile(LANES),
                  full(g_sgu), full(w_cat), full(bias)],
        out_specs=[tile(A_WIDTH)] + [tile(B_WIDTH)] * 3 + [tile(C_WIDTH)] * 3,
        out_shape=[jax.ShapeDtypeStruct((tokens, w), dt) for w, dt in
                   [(A_WIDTH, BF16)] + [(B_WIDTH, BF16)] * 3 + [(C_WIDTH, F32)] * 3],
        scratch_shapes=[pltpu.VMEM((tm, d), BF16)],
        compiler_params=_params("parallel"),
        name="in_projection",
    )(x2d, mod_l, g_mix, w_in, cos, sin, g_sgu, w_cat, bias)
    assert in_width == 2 * A_WIDTH + 3 * B_WIDTH + 3 * C_WIDTH
    return outs


def _stick_kernel(q_ref, k_ref, v_ref, later_ref, o_ref, acc_scr, carry_scr, top_scr, *, blk, lead,
                  n_q):
    first_q = pl.program_id(2) * n_q
    rows = HEADS_PER_TILE * blk
    lane_head = lax.broadcasted_iota(jnp.int32, (blk, LANES), 1) // HEAD_DIM
    t_idx = lax.broadcasted_iota(jnp.int32, (rows, blk), 0) % blk
    before = lax.broadcasted_iota(jnp.int32, (rows, blk), 1) < t_idx

    def stacked_q(j):
        q = q_ref[j * blk:(j + 1) * blk, :]
        return jnp.concatenate(
            [jnp.where(lane_head == h, q, jnp.zeros_like(q)) for h in range(HEADS_PER_TILE)], axis=0)

    def visit(q_stack, first_block, n_blocks, diagonal, carry, acc, out):
        start = pl.multiple_of(first_block * blk, blk)
        k_win = k_ref[pl.ds(start, n_blocks * blk), :]
        v_win = v_ref[pl.ds(start, n_blocks * blk), :]
        z = lax.dot_general(q_stack, k_win, (((1,), (1,)), ((), ())), preferred_element_type=F32)
        yield
        log_beta = jnp.minimum(z, 0.0) - jnp.log(1.0 + jnp.exp(-jnp.abs(z)))
        log_stay = log_beta - z
        parts = []
        for w in range(n_blocks):
            stay_w = log_stay[:, w * blk:(w + 1) * blk]
            if diagonal and w == n_blocks - 1:
                stay_w = jnp.where(before, stay_w, 0.0)
            parts.append(jnp.concatenate(_split_bf16(stay_w), axis=1))
        sums = jnp.dot(jnp.concatenate(parts, axis=0), later_ref[...], preferred_element_type=F32)
        yield
        weights = [None] * n_blocks
        for w in reversed(range(n_blocks)):
            sums_w = sums[w * rows:(w + 1) * rows]
            weight = jnp.exp(log_beta[:, w * blk:(w + 1) * blk] + sums_w[:, :blk] + carry)
            if diagonal and w == n_blocks - 1:
                weight = jnp.where(before, weight, 0.0)
            weights[w] = weight.astype(BF16)
            carry = carry + sums_w[:, blk:]
        acc = acc + jnp.dot(jnp.concatenate(weights, axis=1), v_win, preferred_element_type=F32)
        out.append((carry, acc))

    def run_staged(visits):
        while visits:
            visits = [g for g in visits if next(g, True) is None]

    def first_visits(first_q, n_blocks):
        zero = jnp.zeros((rows, blk), F32)
        outs = [[] for _ in range(n_q)]
        run_staged([visit(stacked_q(j), first_q + j - (n_blocks[j] - 1), n_blocks[j], True,
                          zero, zero, outs[j]) for j in range(n_q)])
        for j in range(n_q):
            carry_scr[j], acc_scr[j] = outs[j][0]
        tops = [jnp.max(outs[j][0][0], axis=0, keepdims=True) for j in range(n_q)]
        top_scr[...] = jnp.broadcast_to(functools.reduce(jnp.maximum, tops), top_scr.shape)

    @pl.when(first_q == 0)
    def _():
        first_visits(0, [min(j + 1, lead) for j in range(n_q)])

    @pl.when(first_q > 0)
    def _():
        first_visits(first_q, [lead] * n_q)

    def alive(carry):
        top = jnp.max(carry, axis=0, keepdims=True)
        return (top[0, 0] > EXP_ZERO_BELOW).astype(jnp.int32)

    def finish(j):
        def cond(state):
            kb, go = state
            return jnp.logical_and(kb >= 0, go > 0)

        def body(state):
            kb, _ = state
            out = []
            run_staged([visit(stacked_q(j), kb, 1, False, carry_scr[j], acc_scr[j], out)])
            carry_scr[j], acc_scr[j] = out[0]
            return kb - 1, alive(carry_scr[j])

        lax.while_loop(cond, body, (first_q + j - lead, alive(carry_scr[j])))

    @pl.when(alive(top_scr[...]) > 0)
    def _():
        for j in range(n_q):
            finish(j)

    for j in range(n_q):
        o_ref[j * blk:(j + 1) * blk, :] = jnp.where(
            lane_head == 0, acc_scr[j, :blk], acc_scr[j, blk:]).astype(BF16)


def _stick_breaking(q, k, v, *, bsz, seq):
    tokens, width = q.shape
    blk = Q_BLOCK
    n_q = STICK_Q_BLOCKS_PER_STEP
    assert n_q >= STICK_LEAD_BLOCKS - 1
    steps = seq // (blk * n_q)
    j_idx = lax.broadcasted_iota(jnp.int32, (2 * blk, 2 * blk), 0) % blk
    s_idx = lax.broadcasted_iota(jnp.int32, (2 * blk, 2 * blk), 1)
    later = jnp.where((j_idx > s_idx) | (s_idx >= blk), 1.0, 0.0).astype(BF16)
    q_spec = pl.BlockSpec((n_q * blk, LANES), lambda b, p, i: (b * steps + i, p))
    kv_spec = pl.BlockSpec((seq, LANES), lambda b, p, i: (b, p))
    return pl.pallas_call(
        functools.partial(_stick_kernel, blk=blk, lead=STICK_LEAD_BLOCKS, n_q=n_q),
        grid=(bsz, width // LANES, steps),
        in_specs=[q_spec, kv_spec, kv_spec, pl.BlockSpec(later.shape, lambda b, p, i: (0, 0))],
        out_specs=q_spec,
        out_shape=jax.ShapeDtypeStruct((tokens, width), BF16),
        scratch_shapes=[pltpu.VMEM((n_q, HEADS_PER_TILE * blk, LANES), F32),
                        pltpu.VMEM((n_q, HEADS_PER_TILE * blk, blk), F32),
                        pltpu.VMEM((8, blk), F32)],
        compiler_params=_params("parallel", "parallel", "parallel"),
        name="stick_breaking",
    )(q, k, v, later)


def _window_kernel(q_ref, k_ref, v_ref, o_ref, lse_ref, *, seq, tq):
    pair = pl.program_id(1)
    rows = HEADS_PER_TILE * WINDOW
    lane_head = lax.broadcasted_iota(jnp.int32, (WINDOW, LANES), 1) // HEAD_DIM
    off = (lax.broadcasted_iota(jnp.int32, (rows, 2 * WINDOW), 1)
           - lax.broadcasted_iota(jnp.int32, (rows, 2 * WINDOW), 0) % WINDOW)
    band = (off >= 0) & (off <= WINDOW)
    from_prev = lax.broadcasted_iota(jnp.int32, (rows, 2 * WINDOW), 1) < WINDOW

    def class_rows(first, d):
        return pl.ds(first, WINDOW) if d == 1 else pl.ds(first, WINDOW, stride=d)

    def block(d, r, j, prev, out):
        cur = class_rows(r + d * j, d)
        q = q_ref[cur, :].astype(BF16)
        k_cur = k_ref[cur, :].astype(BF16)
        v_cur = v_ref[cur, :].astype(BF16)
        out.append((k_cur, v_cur))
        if prev is None:
            before = class_rows(r + d * jnp.maximum(j - WINDOW, 0), d)
            prev = (k_ref[before, :].astype(BF16), v_ref[before, :].astype(BF16))
            valid = band & (jnp.logical_not(from_prev) | (j > 0))
        else:
            valid = band
        q_stack = jnp.concatenate(
            [jnp.where(lane_head == h, q, jnp.zeros_like(q)) for h in range(HEADS_PER_TILE)], axis=0)
        z = lax.dot_general(q_stack, jnp.concatenate([prev[0], k_cur], axis=0),
                            (((1,), (1,)), ((), ())), preferred_element_type=F32)
        yield
        z = jnp.where(valid, z, -jnp.inf)
        z_max = jnp.max(z, axis=-1, keepdims=True)
        prob = jnp.exp(z - z_max)
        denom = jnp.sum(prob, axis=-1, keepdims=True)
        pv = jnp.dot(prob.astype(BF16), jnp.concatenate([prev[1], v_cur], axis=0),
                     preferred_element_type=F32)
        yield
        o = pv * (1.0 / denom)
        lse = z_max + jnp.log(denom)
        o_ref[cur, :] = jnp.where(lane_head == 0, o[:WINDOW], o[WINDOW:])
        lse_ref[cur, :] = jnp.where(lane_head == 0, lse[:WINDOW], lse[WINDOW:])

    def tiles(d):
        per_class = seq // d // tq

        def tile(i, _):
            r = i // per_class
            j0 = (i % per_class) * tq
            visits, prev = [], None
            for s in range(tq // WINDOW):
                out = []
                visits.append(block(d, r, j0 + s * WINDOW, prev, out))
                next(visits[-1])
                prev = out[0]
            while visits:
                visits = [g for g in visits if next(g, True) is None]
            return 0

        lax.fori_loop(0, seq // tq, tile, 0)

    for idx, d in enumerate(DILATIONS):
        pl.when(pair == idx)(functools.partial(tiles, d))


def _dilated_window(q, k, v, *, bsz, seq):
    tokens, width = q.shape
    spec = pl.BlockSpec((seq, LANES), lambda b, p: (b, p))
    return pl.pallas_call(
        functools.partial(_window_kernel, seq=seq, tq=seq // max(DILATIONS)),
        grid=(bsz, width // LANES),
        in_specs=[spec] * 3,
        out_specs=[spec] * 2,
        out_shape=[jax.ShapeDtypeStruct((tokens, width), F32)] * 2,
        compiler_params=_params("parallel", "parallel"),
        name="dilated_window",
    )(q, k, v)


def _layer_tail_kernel(x_ref, mod_ref, ya_ref, yb_ref, oc_ref, lse_ref, wout_ref, g_ref, wup_ref,
                       cw_ref, cb_ref, wdn_ref, gfin_ref, o_ref,
                       y_scr, x_scr, tail_scr, h_scr, up_scr, act_scr, *, tm, tn, per_batch, final):
    i = pl.program_id(0)
    mod = mod_ref[0]
    d_ff = wdn_ref.shape[0]

    y_scr[:, :A_WIDTH] = ya_ref[...]
    y_scr[:, A_WIDTH:A_WIDTH + B_WIDTH] = yb_ref[...]
    pairs = C_WIDTH // LANES
    lse = [lse_ref[:, p * LANES:(p + 1) * LANES] for p in range(pairs)]
    top = functools.reduce(jnp.maximum, lse)
    share = [jnp.exp(l - top) for l in lse]
    inv = 1.0 / functools.reduce(lambda a, b: a + b, share)
    c0 = A_WIDTH + B_WIDTH
    for p in range(pairs):
        o_p = oc_ref[:, p * LANES:(p + 1) * LANES]
        y_scr[:, c0 + p * LANES:c0 + (p + 1) * LANES] = (o_p * (share[p] * inv)).astype(BF16)
    mixed = jnp.dot(y_scr[...], wout_ref[...], preferred_element_type=F32)
    x_scr[...] = x_ref[...] + (1.0 + mod[2:3]) * mixed

    def norm_mod(xv):
        return _rms_scale(xv) * g_ref[...] * (1.0 + mod[4:5]) + mod[3:4]

    opens_sequence = i % per_batch == 0

    @pl.when(opens_sequence)
    def _():
        tail_scr[...] = jnp.zeros_like(tail_scr)

    halo = jnp.where(opens_sequence, 0.0, norm_mod(tail_scr[...]))
    h_scr[:CONV_HALO, :] = halo.astype(BF16)
    h_scr[CONV_HALO:, :] = norm_mod(x_scr[...]).astype(BF16)
    tail_scr[...] = x_scr[tm - CONV_HALO:, :]

    def conv(slot, c0):
        up_scr[slot] = jnp.dot(h_scr[...], wup_ref[:, c0:c0 + tn], preferred_element_type=F32)
        w = cw_ref[:, c0:c0 + tn]
        out = cb_ref[:, c0:c0 + tn]
        for tap in range(CONV_WIDTH):
            shift = CONV_HALO - (CONV_WIDTH - 1) + tap
            out = out + w[tap:tap + 1] * up_scr[slot, shift:shift + tm, :]
        return out

    for n in range(d_ff // tn):
        slot = 2 * (n % (up_scr.shape[0] // 2))
        gate = conv(slot, n * tn)
        val = conv(slot + 1, d_ff + n * tn)
        act = gate * (1.0 / (1.0 + jnp.exp(-gate))) * val
        act_scr[:, n * tn:(n + 1) * tn] = act.astype(BF16)
    down = jnp.dot(act_scr[...], wdn_ref[...], preferred_element_type=F32)
    x_new = x_scr[...] + (1.0 + mod[5:6]) * down
    if final:
        x_new = _rms_scale(x_new) * gfin_ref[...]
    o_ref[...] = x_new


def _layer_tail(x2d, mod_l, y_a, y_b, o_c, lse, w_out, g_ffn, w_up, conv_w, conv_b, w_down, g_final,
                *, seq, tm, tn, final):
    tokens, d = x2d.shape
    per_batch = seq // tm
    tile = lambda w: pl.BlockSpec((tm, w), lambda i: (i, 0))
    full = lambda a: pl.BlockSpec(a.shape, lambda i: (0,) * a.ndim)
    resident = lambda a: pl.BlockSpec(a.shape, lambda i: (0,) * a.ndim,
                                      pipeline_mode=pl.Buffered(1))
    g_ffn = g_ffn.reshape(1, d)
    conv_b = conv_b.reshape(1, -1)
    g_final = g_final.reshape(1, d)
    return pl.pallas_call(
        functools.partial(_layer_tail_kernel, tm=tm, tn=tn, per_batch=per_batch, final=final),
        grid=(tokens // tm,),
        in_specs=[tile(d),
                  pl.BlockSpec((1, N_MOD, d), lambda i: (i // per_batch, 0, 0)),
                  tile(A_WIDTH), tile(B_WIDTH), tile(C_WIDTH), tile(C_WIDTH), resident(w_out),
                  full(g_ffn), resident(w_up), full(conv_w), full(conv_b), resident(w_down),
                  full(g_final)],
        out_specs=tile(d),
        out_shape=jax.ShapeDtypeStruct((tokens, d), F32),
        scratch_shapes=[pltpu.VMEM((tm, w_out.shape[0]), BF16),
                        pltpu.VMEM((tm, d), F32),
                        pltpu.VMEM((CONV_HALO, d), F32),
                        pltpu.VMEM((tm + CONV_HALO, d), BF16),
                        pltpu.VMEM((4, tm + CONV_HALO, tn), F32),
                        pltpu.VMEM((tm, w_down.shape[0]), BF16)],
        compiler_params=_params("arbitrary"),
        name="layer_tail",
    )(x2d, mod_l, y_a, y_b, o_c, lse, w_out, g_ffn, w_up, conv_w, conv_b, w_down, g_final)


def kernel(x, c, positions, w_ada, b_ada, g_mix, w_in, g_sgu, w_sp, b_sp, w_out, g_ffn, w_up,
           conv_w, conv_b, w_down, g_final):
    bsz, seq, d = x.shape
    depth = w_ada.shape[0]
    tm = min(512, seq)
    assert seq % tm == 0 and seq % (max(DILATIONS) * WINDOW) == 0
    mod = _modulation(c, w_ada, b_ada)
    cos, sin = _rope_tables(positions)
    cos = cos.reshape(bsz * seq, LANES)
    sin = sin.reshape(bsz * seq, LANES)
    x2d = x.reshape(bsz * seq, d)
    for l in range(depth):
        y_a, b_q, b_k, b_v, c_q, c_k, c_v = _in_projection(
            x2d, mod[l], g_mix[l], w_in[l].astype(BF16), cos, sin, g_sgu[l], w_sp[l], b_sp[l],
            seq=seq, tm=tm)
        y_b = _stick_breaking(b_q, b_k, b_v, bsz=bsz, seq=seq)
        o_c, lse = _dilated_window(c_q, c_k, c_v, bsz=bsz, seq=seq)
        x2d = _layer_tail(x2d, mod[l], y_a, y_b, o_c, lse, w_out[l].astype(BF16), g_ffn[l],
                          w_up[l].astype(BF16), conv_w[l], conv_b[l], w_down[l].astype(BF16),
                          g_final, seq=seq, tm=tm, tn=256, final=(l == depth - 1))
    return x2d.reshape(bsz, seq, d)
```

```python
import functools

import jax
import jax.numpy as jnp
from jax import lax
from jax.experimental import pallas as pl
from jax.experimental.pallas import tpu as pltpu

F32 = jnp.float32
BF16 = jnp.bfloat16

LANES = 128
MXU_COLUMNS = 256
HEAD_DIM = 64
HEADS_PER_TILE = LANES // HEAD_DIM
A_GROUPS = 4
A_WIDTH = A_GROUPS * HEAD_DIM
CHUNK = 128
B_WIDTH = 6 * HEAD_DIM
C_WIDTH = 6 * HEAD_DIM
DILATIONS = (1, 4, 16)
WINDOW = 128
WINDOW_TILE_ROWS = 512
Q_BLOCK = 128
STICK_LEAD_BLOCKS = 3
STICK_Q_BLOCKS_PER_STEP = 8
CONV_WIDTH = 3
CONV_HALO = 8
ROPE_THETA = 10000.0
EPS = 1e-6
N_MOD = 6
EXP_ZERO_BELOW = -104.0
VMEM_LIMIT_BYTES = 56 * 1024 * 1024


def _params(*semantics):
    return pltpu.CompilerParams(dimension_semantics=semantics,
                                vmem_limit_bytes=VMEM_LIMIT_BYTES)


def _layer_weight(stacked, layer):
    return pl.BlockSpec((None,) + stacked.shape[1:], lambda i: (layer, 0, 0),
                        pipeline_mode=pl.Buffered(1))


def _split_bf16(a):
    hi = a.astype(BF16)
    lo = (a - hi.astype(F32)).astype(BF16)
    return hi, lo


def _rms_scale(x):
    return x * lax.rsqrt(jnp.mean(x * x, axis=-1, keepdims=True) + EPS)


def _mod_kernel(c_ref, w_ref, b_ref, o_ref):
    c = c_ref[...]
    c_act = c * (1.0 / (1.0 + jnp.exp(-c)))
    o_ref[0] = jnp.dot(c_act, w_ref[0], preferred_element_type=F32) + b_ref[0]


def _modulation(c, w_ada, b_ada):
    depth, d, nd = w_ada.shape
    bsz = c.shape[0]
    rows = -(-bsz // 8) * 8
    c_pad = jnp.zeros((rows, d), F32).at[:bsz].set(c)
    out = pl.pallas_call(
        _mod_kernel,
        grid=(depth, nd // d),
        in_specs=[pl.BlockSpec((rows, d), lambda l, j: (0, 0)),
                  pl.BlockSpec((1, d, d), lambda l, j: (l, 0, j)),
                  pl.BlockSpec((1, 1, d), lambda l, j: (l, 0, j))],
        out_specs=pl.BlockSpec((1, rows, d), lambda l, j: (l, 0, j)),
        out_shape=jax.ShapeDtypeStruct((depth, rows, nd), F32),
        compiler_params=_params("parallel", "parallel"),
        name="adaln_modulation",
    )(c_pad, w_ada, b_ada.reshape(depth, 1, nd))
    return out[:, :bsz].reshape(depth, bsz, N_MOD, d)


def _rope_kernel(pos_ref, freq_ref, sign_ref, cos_ref, sin_ref):
    ang = pos_ref[0] * freq_ref[...]
    cos_ref[0] = jnp.cos(ang)
    sin_ref[0] = jnp.sin(ang) * sign_ref[...]


def _rope_tables(positions):
    bsz, seq = positions.shape
    half = HEAD_DIM // 2
    inv_freq = ROPE_THETA ** (-jnp.arange(0, HEAD_DIM, 2, dtype=F32) / HEAD_DIM)
    freq = jnp.tile(inv_freq, LANES // half).reshape(1, LANES)
    sign = jnp.tile(jnp.concatenate([-jnp.ones((half,), F32), jnp.ones((half,), F32)]),
                    HEADS_PER_TILE).reshape(1, LANES)
    pos = positions.astype(F32).reshape(bsz, seq, 1)
    row = pl.BlockSpec((1, LANES), lambda b: (0, 0))
    tab = pl.BlockSpec((1, seq, LANES), lambda b: (b, 0, 0))
    return pl.pallas_call(
        _rope_kernel,
        grid=(bsz,),
        in_specs=[pl.BlockSpec((1, seq, 1), lambda b: (b, 0, 0)), row, row],
        out_specs=[tab, tab],
        out_shape=[jax.ShapeDtypeStruct((bsz, seq, LANES), F32)] * 2,
        compiler_params=_params("parallel"),
        name="rope_tables",
    )(pos, freq, sign)


def _inproj_kernel(x_ref, mod_ref, g_ref, w_ref, cos_ref, sin_ref, gsgu_ref, wsp_ref, bsp_ref,
                   ya_ref, bq_ref, bk_ref, bv_ref, cq_ref, ck_ref, cv_ref, h_scr, *, tm):
    mod = mod_ref[0]
    h = _rms_scale(x_ref[...]) * g_ref[...] * (1.0 + mod[1:2]) + mod[0:1]
    h_scr[...] = h.astype(BF16)
    scale = HEAD_DIM ** -0.5

    def proj(c0, width):
        return jnp.dot(h_scr[...], w_ref[:, c0:c0 + width], preferred_element_type=F32)

    pa = proj(0, 2 * A_WIDTH)
    u = jax.nn.gelu(pa[:, :A_WIDTH])
    v = jax.nn.gelu(pa[:, A_WIDTH:])
    grp_r = lax.broadcasted_iota(jnp.int32, (A_WIDTH, A_WIDTH), 0) // HEAD_DIM
    grp_c = lax.broadcasted_iota(jnp.int32, (A_WIDTH, A_WIDTH), 1) // HEAD_DIM
    group_ones = jnp.where(grp_r == grp_c, 1.0, 0.0).astype(BF16)
    hi, lo = _split_bf16(v * v)
    ssq = (jnp.dot(hi, group_ones, preferred_element_type=F32)
           + jnp.dot(lo, group_ones, preferred_element_type=F32))
    vn = v * lax.rsqrt(ssq * (1.0 / HEAD_DIM) + EPS) * gsgu_ref[...]
    t_idx = lax.broadcasted_iota(jnp.int32, (CHUNK, A_GROUPS * CHUNK), 0)
    s_idx = lax.broadcasted_iota(jnp.int32, (CHUNK, A_GROUPS * CHUNK), 1) % CHUNK
    w_causal = jnp.where(s_idx <= t_idx, wsp_ref[...], 0.0).astype(BF16)
    lane_grp = lax.broadcasted_iota(jnp.int32, (CHUNK, A_WIDTH), 1) // HEAD_DIM
    for c in range(tm // CHUNK):
        rows = slice(c * CHUNK, (c + 1) * CHUNK)
        vc = vn[rows]
        stacked = jnp.concatenate(
            [jnp.where(lane_grp == g, vc, 0.0) for g in range(A_GROUPS)], axis=0).astype(BF16)
        mixed = jnp.dot(w_causal, stacked, preferred_element_type=F32) + bsp_ref[...]
        ya_ref[rows, :] = (u[rows] * mixed).astype(BF16)

    cos = cos_ref[...]
    sin = sin_ref[...]
    first_half = (lax.broadcasted_iota(jnp.int32, (tm, LANES), 1) % HEAD_DIM) < HEAD_DIM // 2

    def rope(pc):
        outs = []
        for p in range(C_WIDTH // LANES):
            xp = pc[:, p * LANES:(p + 1) * LANES]
            rot = jnp.where(first_half,
                            pltpu.roll(xp, LANES - HEAD_DIM // 2, axis=1),
                            pltpu.roll(xp, HEAD_DIM // 2, axis=1))
            outs.append(xp * cos + rot * sin)
        return jnp.concatenate(outs, axis=1)

    assert B_WIDTH == C_WIDTH and (2 * B_WIDTH) % MXU_COLUMNS == 0
    c0 = 2 * A_WIDTH
    qk = proj(c0, 2 * B_WIDTH)
    bq_ref[...] = (qk[:, :B_WIDTH] * scale).astype(BF16)
    bk_ref[...] = qk[:, B_WIDTH:].astype(BF16)
    vq = proj(c0 + 2 * B_WIDTH, 2 * B_WIDTH)
    bv_ref[...] = vq[:, :B_WIDTH].astype(BF16)
    cq_ref[...] = rope(vq[:, B_WIDTH:]) * scale
    kv = proj(c0 + 4 * B_WIDTH, 2 * B_WIDTH)
    ck_ref[...] = rope(kv[:, :C_WIDTH])
    cv_ref[...] = kv[:, C_WIDTH:]


def _in_projection(x2d, mod_l, g_mix, w_in, cos, sin, g_sgu, w_sp, b_sp, *, layer, seq, tm):
    tokens, d = x2d.shape
    per_batch = seq // tm
    in_width = w_in.shape[-1]
    w_cat = jnp.transpose(w_sp, (1, 0, 2)).reshape(CHUNK, A_GROUPS * CHUNK)
    bias = jnp.repeat(b_sp.T, HEAD_DIM, axis=1)
    tile = lambda w: pl.BlockSpec((tm, w), lambda i: (i, 0))
    full = lambda a: pl.BlockSpec(a.shape, lambda i: (0,) * a.ndim)
    g_mix = g_mix.reshape(1, d)
    g_sgu = g_sgu.reshape(1, A_WIDTH)
    outs = pl.pallas_call(
        functools.partial(_inproj_kernel, tm=tm),
        grid=(tokens // tm,),
        in_specs=[tile(d),
                  pl.BlockSpec((1, N_MOD, d), lambda i: (i // per_batch, 0, 0)),
                  full(g_mix), _layer_weight(w_in, layer), tile(LANES), tile(LANES),
                  full(g_sgu), full(w_cat), full(bias)],
        out_specs=[tile(A_WIDTH)] + [tile(B_WIDTH)] * 3 + [tile(C_WIDTH)] * 3,
        out_shape=[jax.ShapeDtypeStruct((tokens, w), dt) for w, dt in
                   [(A_WIDTH, BF16)] + [(B_WIDTH, BF16)] * 3 + [(C_WIDTH, F32)] * 3],
        scratch_shapes=[pltpu.VMEM((tm, d), BF16)],
        compiler_params=_params("parallel"),
        name="in_projection",
    )(x2d, mod_l, g_mix, w_in, cos, sin, g_sgu, w_cat, bias)
    assert in_width == 2 * A_WIDTH + 3 * B_WIDTH + 3 * C_WIDTH
    return outs


def _stick_kernel(q_ref, k_ref, v_ref, later_ref, o_ref, acc_scr, carry_scr, top_scr, *, blk, lead,
                  n_q):
    first_q = pl.program_id(2) * n_q
    rows = HEADS_PER_TILE * blk
    lane_head = lax.broadcasted_iota(jnp.int32, (blk, LANES), 1) // HEAD_DIM
    t_idx = lax.broadcasted_iota(jnp.int32, (rows, blk), 0) % blk
    before = lax.broadcasted_iota(jnp.int32, (rows, blk), 1) < t_idx

    def stacked_q(j):
        q = q_ref[j * blk:(j + 1) * blk, :]
        return jnp.concatenate(
            [jnp.where(lane_head == h, q, jnp.zeros_like(q)) for h in range(HEADS_PER_TILE)], axis=0)

    def visit(q_stack, first_block, n_blocks, diagonal, carry, acc, out):
        start = pl.multiple_of(first_block * blk, blk)
        k_win = k_ref[pl.ds(start, n_blocks * blk), :]
        v_win = v_ref[pl.ds(start, n_blocks * blk), :]
        z = lax.dot_general(q_stack, k_win, (((1,), (1,)), ((), ())), preferred_element_type=F32)
        yield
        log_beta = jnp.minimum(z, 0.0) - jnp.log(1.0 + jnp.exp(-jnp.abs(z)))
        log_stay = log_beta - z
        parts = []
        for w in range(n_blocks):
            stay_w = log_stay[:, w * blk:(w + 1) * blk]
            if diagonal and w == n_blocks - 1:
                stay_w = jnp.where(before, stay_w, 0.0)
            parts.append(jnp.concatenate(_split_bf16(stay_w), axis=1))
        sums = jnp.dot(jnp.concatenate(parts, axis=0), later_ref[...], preferred_element_type=F32)
        yield
        weights = [None] * n_blocks
        for w in reversed(range(n_blocks)):
            sums_w = sums[w * rows:(w + 1) * rows]
            weight = jnp.exp(log_beta[:, w * blk:(w + 1) * blk] + sums_w[:, :blk] + carry)
            if diagonal and w == n_blocks - 1:
                weight = jnp.where(before, weight, 0.0)
            weights[w] = weight.astype(BF16)
            carry = carry + sums_w[:, blk:]
        acc = acc + jnp.dot(jnp.concatenate(weights, axis=1), v_win, preferred_element_type=F32)
        out.append((carry, acc))

    def run_staged(visits):
        while visits:
            visits = [g for g in visits if next(g, True) is None]

    def first_visits(first_q, n_blocks):
        zero = jnp.zeros((rows, blk), F32)
        outs = [[] for _ in range(n_q)]
        run_staged([visit(stacked_q(j), first_q + j - (n_blocks[j] - 1), n_blocks[j], True,
                          zero, zero, outs[j]) for j in range(n_q)])
        for j in range(n_q):
            carry_scr[j], acc_scr[j] = outs[j][0]
        tops = [jnp.max(outs[j][0][0], axis=0, keepdims=True) for j in range(n_q)]
        top_scr[...] = jnp.broadcast_to(functools.reduce(jnp.maximum, tops), top_scr.shape)

    @pl.when(first_q == 0)
    def _():
        first_visits(0, [min(j + 1, lead) for j in range(n_q)])

    @pl.when(first_q > 0)
    def _():
        first_visits(first_q, [lead] * n_q)

    def alive(carry):
        top = jnp.max(carry, axis=0, keepdims=True)
        return (top[0, 0] > EXP_ZERO_BELOW).astype(jnp.int32)

    def finish(j):
        def cond(state):
            kb, go = state
            return jnp.logical_and(kb >= 0, go > 0)

        def body(state):
            kb, _ = state
            out = []
            run_staged([visit(stacked_q(j), kb, 1, False, carry_scr[j], acc_scr[j], out)])
            carry_scr[j], acc_scr[j] = out[0]
            return kb - 1, alive(carry_scr[j])

        lax.while_loop(cond, body, (first_q + j - lead, alive(carry_scr[j])))

    @pl.when(alive(top_scr[...]) > 0)
    def _():
        for j in range(n_q):
            finish(j)

    for j in range(n_q):
        o_ref[j * blk:(j + 1) * blk, :] = jnp.where(
            lane_head == 0, acc_scr[j, :blk], acc_scr[j, blk:]).astype(BF16)


def _stick_breaking(q, k, v, *, bsz, seq):
    tokens, width = q.shape
    blk = Q_BLOCK
    n_q = STICK_Q_BLOCKS_PER_STEP
    assert n_q >= STICK_LEAD_BLOCKS - 1
    steps = seq // (blk * n_q)
    j_idx = lax.broadcasted_iota(jnp.int32, (2 * blk, 2 * blk), 0) % blk
    s_idx = lax.broadcasted_iota(jnp.int32, (2 * blk, 2 * blk), 1)
    later = jnp.where((j_idx > s_idx) | (s_idx >= blk), 1.0, 0.0).astype(BF16)
    q_spec = pl.BlockSpec((n_q * blk, LANES), lambda b, p, i: (b * steps + i, p))
    kv_spec = pl.BlockSpec((seq, LANES), lambda b, p, i: (b, p))
    return pl.pallas_call(
        functools.partial(_stick_kernel, blk=blk, lead=STICK_LEAD_BLOCKS, n_q=n_q),
        grid=(bsz, width // LANES, steps),
        in_specs=[q_spec, kv_spec, kv_spec, pl.BlockSpec(later.shape, lambda b, p, i: (0, 0))],
        out_specs=q_spec,
        out_shape=jax.ShapeDtypeStruct((tokens, width), BF16),
        scratch_shapes=[pltpu.VMEM((n_q, HEADS_PER_TILE * blk, LANES), F32),
                        pltpu.VMEM((n_q, HEADS_PER_TILE * blk, blk), F32),
                        pltpu.VMEM((8, blk), F32)],
        compiler_params=_params("parallel", "parallel", "parallel"),
        name="stick_breaking",
    )(q, k, v, later)


def _window_kernel(q_ref, k_ref, v_ref, o_ref, lse_ref, *, seq, max_tq):
    pair = pl.program_id(1)
    rows = HEADS_PER_TILE * WINDOW
    lane_head = lax.broadcasted_iota(jnp.int32, (WINDOW, LANES), 1) // HEAD_DIM
    off = (lax.broadcasted_iota(jnp.int32, (rows, 2 * WINDOW), 1)
           - lax.broadcasted_iota(jnp.int32, (rows, 2 * WINDOW), 0) % WINDOW)
    band = (off >= 0) & (off <= WINDOW)
    from_prev = lax.broadcasted_iota(jnp.int32, (rows, 2 * WINDOW), 1) < WINDOW

    def class_rows(first, d):
        return pl.ds(first, WINDOW) if d == 1 else pl.ds(first, WINDOW, stride=d)

    def block(d, r, j, prev, out):
        cur = class_rows(r + d * j, d)
        q = q_ref[cur, :].astype(BF16)
        k_cur = k_ref[cur, :].astype(BF16)
        v_cur = v_ref[cur, :].astype(BF16)
        out.append((k_cur, v_cur))
        if prev is None:
            before = class_rows(r + d * jnp.maximum(j - WINDOW, 0), d)
            prev = (k_ref[before, :].astype(BF16), v_ref[before, :].astype(BF16))
            valid = band & (jnp.logical_not(from_prev) | (j > 0))
        else:
            valid = band
        q_stack = jnp.concatenate(
            [jnp.where(lane_head == h, q, jnp.zeros_like(q)) for h in range(HEADS_PER_TILE)], axis=0)
        z = lax.dot_general(q_stack, jnp.concatenate([prev[0], k_cur], axis=0),
                            (((1,), (1,)), ((), ())), preferred_element_type=F32)
        yield
        z = jnp.where(valid, z, -jnp.inf)
        z_max = jnp.max(z, axis=-1, keepdims=True)
        prob = jnp.exp(z - z_max)
        denom = jnp.sum(prob, axis=-1, keepdims=True)
        pv = jnp.dot(prob.astype(BF16), jnp.concatenate([prev[1], v_cur], axis=0),
                     preferred_element_type=F32)
        yield
        o = pv * (1.0 / denom)
        lse = z_max + jnp.log(denom)
        o_ref[cur, :] = jnp.where(lane_head == 0, o[:WINDOW], o[WINDOW:])
        lse_ref[cur, :] = jnp.where(lane_head == 0, lse[:WINDOW], lse[WINDOW:])

    def tiles(d):
        tq = min(max_tq, seq // d)
        runs = max_tq // tq
        per_class = seq // d // tq

        def tile(i, _):
            visits = []
            for c in range(runs):
                run = i * runs + c
                r = run // per_class
                j0 = (run % per_class) * tq
                prev = None
                for s in range(tq // WINDOW):
                    out = []
                    visits.append(block(d, r, j0 + s * WINDOW, prev, out))
                    next(visits[-1])
                    prev = out[0]
            while visits:
                visits = [g for g in visits if next(g, True) is None]
            return 0

        lax.fori_loop(0, seq // max_tq, tile, 0)

    for idx, d in enumerate(DILATIONS):
        pl.when(pair == idx)(functools.partial(tiles, d))


def _dilated_window(q, k, v, *, bsz, seq):
    tokens, width = q.shape
    spec = pl.BlockSpec((seq, LANES), lambda b, p: (b, p))
    return pl.pallas_call(
        functools.partial(_window_kernel, seq=seq, max_tq=WINDOW_TILE_ROWS),
        grid=(bsz, width // LANES),
        in_specs=[spec] * 3,
        out_specs=[spec] * 2,
        out_shape=[jax.ShapeDtypeStruct((tokens, width), F32)] * 2,
        compiler_params=_params("parallel", "parallel"),
        name="dilated_window",
    )(q, k, v)


def _layer_tail_kernel(x_ref, mod_ref, ya_ref, yb_ref, oc_ref, lse_ref, wout_ref, g_ref, wup_ref,
                       cw_ref, cb_ref, wdn_ref, gfin_ref, o_ref,
                       y_scr, x_scr, tail_scr, h_scr, up_scr, act_scr, down_scr, *, tm, tn, per_batch,
                       final):
    i = pl.program_id(0)
    mod = mod_ref[0]
    d_ff = wdn_ref.shape[0]

    y_scr[:, :A_WIDTH] = ya_ref[...]
    y_scr[:, A_WIDTH:A_WIDTH + B_WIDTH] = yb_ref[...]
    pairs = C_WIDTH // LANES
    lse = [lse_ref[:, p * LANES:(p + 1) * LANES] for p in range(pairs)]
    top = functools.reduce(jnp.maximum, lse)
    share = [jnp.exp(l - top) for l in lse]
    inv = 1.0 / functools.reduce(lambda a, b: a + b, share)
    c0 = A_WIDTH + B_WIDTH
    for p in range(pairs):
        o_p = oc_ref[:, p * LANES:(p + 1) * LANES]
        y_scr[:, c0 + p * LANES:c0 + (p + 1) * LANES] = (o_p * (share[p] * inv)).astype(BF16)
    mixed = jnp.dot(y_scr[...], wout_ref[...], preferred_element_type=F32)
    x_scr[...] = x_ref[...] + (1.0 + mod[2:3]) * mixed

    def norm_mod(xv):
        return _rms_scale(xv) * g_ref[...] * (1.0 + mod[4:5]) + mod[3:4]

    opens_sequence = i % per_batch == 0

    @pl.when(opens_sequence)
    def _():
        tail_scr[...] = jnp.zeros_like(tail_scr)

    halo = jnp.where(opens_sequence, 0.0, norm_mod(tail_scr[...]))
    h_scr[:CONV_HALO, :] = halo.astype(BF16)
    h_scr[CONV_HALO:, :] = norm_mod(x_scr[...]).astype(BF16)
    tail_scr[...] = x_scr[tm - CONV_HALO:, :]

    half = tm // 2

    def conv(slot, c0):
        up = jnp.dot(h_scr[...], wup_ref[:, c0:c0 + tn], preferred_element_type=F32)
        w = cw_ref[:, c0:c0 + tn]
        bias = cb_ref[:, c0:c0 + tn]
        slabs = range(tn // LANES)
        for s in slabs:
            up_scr[slot, s] = up[:, s * LANES:(s + 1) * LANES]
        parities = []
        for parity in range(2):
            cols = []
            for s in slabs:
                lanes = slice(s * LANES, (s + 1) * LANES)
                out = bias[:, lanes]
                for tap in range(CONV_WIDTH):
                    first = CONV_HALO - (CONV_WIDTH - 1) + tap + parity
                    out = out + w[tap:tap + 1, lanes] * up_scr[slot, s, pl.ds(first, half, stride=2), :]
                cols.append(out)
            parities.append(jnp.concatenate(cols, axis=1))
        return jnp.concatenate(parities, axis=0)

    for n in range(d_ff // tn):
        slot = 2 * (n % (up_scr.shape[0] // 2))
        gate = conv(slot, n * tn)
        val = conv(slot + 1, d_ff + n * tn)
        act = gate * (1.0 / (1.0 + jnp.exp(-gate))) * val
        act_scr[:, n * tn:(n + 1) * tn] = act.astype(BF16)
    down = jnp.dot(act_scr[...], wdn_ref[...], preferred_element_type=F32)
    for s in range(down_scr.shape[0]):
        lanes = slice(s * LANES, (s + 1) * LANES)
        down_scr[s, pl.ds(0, half, stride=2), :] = down[:half, lanes]
        down_scr[s, pl.ds(1, half, stride=2), :] = down[half:, lanes]
    down = jnp.concatenate([down_scr[s] for s in range(down_scr.shape[0])], axis=1)
    x_new = x_scr[...] + (1.0 + mod[5:6]) * down
    if final:
        x_new = _rms_scale(x_new) * gfin_ref[...]
    o_ref[...] = x_new


def _layer_tail(x2d, mod_l, y_a, y_b, o_c, lse, w_out, g_ffn, w_up, conv_w, conv_b, w_down, g_final,
                *, layer, seq, tm, tn, final):
    tokens, d = x2d.shape
    per_batch = seq // tm
    tile = lambda w: pl.BlockSpec((tm, w), lambda i: (i, 0))
    full = lambda a: pl.BlockSpec(a.shape, lambda i: (0,) * a.ndim)
    resident = lambda a: _layer_weight(a, layer)
    g_ffn = g_ffn.reshape(1, d)
    conv_b = conv_b.reshape(1, -1)
    g_final = g_final.reshape(1, d)
    return pl.pallas_call(
        functools.partial(_layer_tail_kernel, tm=tm, tn=tn, per_batch=per_batch, final=final),
        grid=(tokens // tm,),
        in_specs=[tile(d),
                  pl.BlockSpec((1, N_MOD, d), lambda i: (i // per_batch, 0, 0)),
                  tile(A_WIDTH), tile(B_WIDTH), tile(C_WIDTH), tile(C_WIDTH), resident(w_out),
                  full(g_ffn), resident(w_up), full(conv_w), full(conv_b), resident(w_down),
                  full(g_final)],
        out_specs=tile(d),
        out_shape=jax.ShapeDtypeStruct((tokens, d), F32),
        scratch_shapes=[pltpu.VMEM((tm, w_out.shape[1]), BF16),
                        pltpu.VMEM((tm, d), F32),
                        pltpu.VMEM((CONV_HALO, d), F32),
                        pltpu.VMEM((tm + CONV_HALO, d), BF16),
                        pltpu.VMEM((4, tn // LANES, tm + CONV_HALO, LANES), F32),
                        pltpu.VMEM((tm, w_down.shape[1]), BF16),
                        pltpu.VMEM((d // LANES, tm, LANES), F32)],
        compiler_params=_params("arbitrary"),
        name="layer_tail",
    )(x2d, mod_l, y_a, y_b, o_c, lse, w_out, g_ffn, w_up, conv_w, conv_b, w_down, g_final)


def kernel(x, c, positions, w_ada, b_ada, g_mix, w_in, g_sgu, w_sp, b_sp, w_out, g_ffn, w_up,
           conv_w, conv_b, w_down, g_final):
    bsz, seq, d = x.shape
    depth = w_ada.shape[0]
    tm = min(512, seq)
    assert seq % tm == 0 and seq % (max(DILATIONS) * WINDOW) == 0
    mod = _modulation(c, w_ada, b_ada)
    cos, sin = _rope_tables(positions)
    cos = cos.reshape(bsz * seq, LANES)
    sin = sin.reshape(bsz * seq, LANES)
    x2d = x.reshape(bsz * seq, d)
    w_in, w_out, w_up, w_down = (w.astype(BF16) for w in (w_in, w_out, w_up, w_down))
    for l in range(depth):
        y_a, b_q, b_k, b_v, c_q, c_k, c_v = _in_projection(
            x2d, mod[l], g_mix[l], w_in, cos, sin, g_sgu[l], w_sp[l], b_sp[l],
            layer=l, seq=seq, tm=tm)
        y_b = _stick_breaking(b_q, b_k, b_v, bsz=bsz, seq=seq)
        o_c, lse = _dilated_window(c_q, c_k, c_v, bsz=bsz, seq=seq)
        x2d = _layer_tail(x2d, mod[l], y_a, y_b, o_c, lse, w_out, g_ffn[l], w_up, conv_w[l],
                          conv_b[l], w_down, g_final, layer=l, seq=seq, tm=tm, tn=256,
                          final=(l == depth - 1))
    return x2d.reshape(bsz, seq, d)
```

```python
import functools

import jax
import jax.numpy as jnp
from jax import lax
from jax.experimental import pallas as pl
from jax.experimental.pallas import tpu as pltpu

F32 = jnp.float32
BF16 = jnp.bfloat16

LANES = 128
MXU_COLUMNS = 256
HEAD_DIM = 64
HEADS_PER_TILE = LANES // HEAD_DIM
A_GROUPS = 4
A_WIDTH = A_GROUPS * HEAD_DIM
CHUNK = 128
B_WIDTH = 6 * HEAD_DIM
C_WIDTH = 6 * HEAD_DIM
DILATIONS = (1, 4, 16)
WINDOW = 128
WINDOW_TILE_ROWS = 512
Q_BLOCK = 128
STICK_LEAD_BLOCKS = 3
STICK_Q_BLOCKS_PER_STEP = 8
CONV_WIDTH = 3
CONV_HALO = 8
ROPE_THETA = 10000.0
EPS = 1e-6
N_MOD = 6
LOG2_E = 1.4426950408889634
EXP2_ZERO_BELOW = -151.0
VMEM_LIMIT_BYTES = 56 * 1024 * 1024


def _params(*semantics):
    return pltpu.CompilerParams(dimension_semantics=semantics,
                                vmem_limit_bytes=VMEM_LIMIT_BYTES)


def _layer_weight(stacked, layer):
    return pl.BlockSpec((None,) + stacked.shape[1:], lambda i: (layer, 0, 0),
                        pipeline_mode=pl.Buffered(1))


def _split_bf16(a):
    hi = a.astype(BF16)
    lo = (a - hi.astype(F32)).astype(BF16)
    return hi, lo


def _rms_scale(x):
    return x * lax.rsqrt(jnp.mean(x * x, axis=-1, keepdims=True) + EPS)


def _mod_kernel(c_ref, w_ref, b_ref, o_ref):
    c = c_ref[...]
    c_act = c * (1.0 / (1.0 + jnp.exp(-c)))
    o_ref[0] = jnp.dot(c_act, w_ref[0], preferred_element_type=F32) + b_ref[0]


def _modulation(c, w_ada, b_ada):
    depth, d, nd = w_ada.shape
    bsz = c.shape[0]
    rows = -(-bsz // 8) * 8
    c_pad = jnp.zeros((rows, d), F32).at[:bsz].set(c)
    out = pl.pallas_call(
        _mod_kernel,
        grid=(depth, nd // d),
        in_specs=[pl.BlockSpec((rows, d), lambda l, j: (0, 0)),
                  pl.BlockSpec((1, d, d), lambda l, j: (l, 0, j)),
                  pl.BlockSpec((1, 1, d), lambda l, j: (l, 0, j))],
        out_specs=pl.BlockSpec((1, rows, d), lambda l, j: (l, 0, j)),
        out_shape=jax.ShapeDtypeStruct((depth, rows, nd), F32),
        compiler_params=_params("parallel", "parallel"),
        name="adaln_modulation",
    )(c_pad, w_ada, b_ada.reshape(depth, 1, nd))
    return out[:, :bsz].reshape(depth, bsz, N_MOD, d)


def _rope_kernel(pos_ref, freq_ref, sign_ref, cos_ref, sin_ref):
    ang = pos_ref[0] * freq_ref[...]
    cos_ref[0] = jnp.cos(ang)
    sin_ref[0] = jnp.sin(ang) * sign_ref[...]


def _rope_tables(positions):
    bsz, seq = positions.shape
    half = HEAD_DIM // 2
    inv_freq = ROPE_THETA ** (-jnp.arange(0, HEAD_DIM, 2, dtype=F32) / HEAD_DIM)
    freq = jnp.tile(inv_freq, LANES // half).reshape(1, LANES)
    sign = jnp.tile(jnp.concatenate([-jnp.ones((half,), F32), jnp.ones((half,), F32)]),
                    HEADS_PER_TILE).reshape(1, LANES)
    pos = positions.astype(F32).reshape(bsz, seq, 1)
    row = pl.BlockSpec((1, LANES), lambda b: (0, 0))
    tab = pl.BlockSpec((1, seq, LANES), lambda b: (b, 0, 0))
    return pl.pallas_call(
        _rope_kernel,
        grid=(bsz,),
        in_specs=[pl.BlockSpec((1, seq, 1), lambda b: (b, 0, 0)), row, row],
        out_specs=[tab, tab],
        out_shape=[jax.ShapeDtypeStruct((bsz, seq, LANES), F32)] * 2,
        compiler_params=_params("parallel"),
        name="rope_tables",
    )(pos, freq, sign)


def _inproj_kernel(x_ref, mod_ref, g_ref, w_ref, cos_ref, sin_ref, gsgu_ref, wsp_ref, bsp_ref,
                   ya_ref, bq_ref, bk_ref, bv_ref, cq_ref, ck_ref, cv_ref, h_scr, *, tm):
    mod = mod_ref[0]
    h = _rms_scale(x_ref[...]) * g_ref[...] * (1.0 + mod[1:2]) + mod[0:1]
    h_scr[...] = h.astype(BF16)
    scale = HEAD_DIM ** -0.5

    def proj(c0, width):
        return jnp.dot(h_scr[...], w_ref[:, c0:c0 + width], preferred_element_type=F32)

    assert B_WIDTH == C_WIDTH and (2 * B_WIDTH) % MXU_COLUMNS == 0
    pa = proj(0, 2 * A_WIDTH)
    c0 = 2 * A_WIDTH
    qk = proj(c0, 2 * B_WIDTH)
    vq = proj(c0 + 2 * B_WIDTH, 2 * B_WIDTH)
    kv = proj(c0 + 4 * B_WIDTH, 2 * B_WIDTH)
    u = jax.nn.gelu(pa[:, :A_WIDTH])
    v = jax.nn.gelu(pa[:, A_WIDTH:])
    grp_r = lax.broadcasted_iota(jnp.int32, (A_WIDTH, A_WIDTH), 0) // HEAD_DIM
    grp_c = lax.broadcasted_iota(jnp.int32, (A_WIDTH, A_WIDTH), 1) // HEAD_DIM
    group_ones = jnp.where(grp_r == grp_c, 1.0, 0.0).astype(BF16)
    hi, lo = _split_bf16(v * v)
    ssq = (jnp.dot(hi, group_ones, preferred_element_type=F32)
           + jnp.dot(lo, group_ones, preferred_element_type=F32))
    vn = v * lax.rsqrt(ssq * (1.0 / HEAD_DIM) + EPS) * gsgu_ref[...]
    t_idx = lax.broadcasted_iota(jnp.int32, (CHUNK, A_GROUPS * CHUNK), 0)
    s_idx = lax.broadcasted_iota(jnp.int32, (CHUNK, A_GROUPS * CHUNK), 1) % CHUNK
    w_causal = jnp.where(s_idx <= t_idx, wsp_ref[...], 0.0).astype(BF16)
    lane_grp = lax.broadcasted_iota(jnp.int32, (CHUNK, A_WIDTH), 1) // HEAD_DIM
    for c in range(tm // CHUNK):
        rows = slice(c * CHUNK, (c + 1) * CHUNK)
        vc = vn[rows]
        stacked = jnp.concatenate(
            [jnp.where(lane_grp == g, vc, 0.0) for g in range(A_GROUPS)], axis=0).astype(BF16)
        mixed = jnp.dot(w_causal, stacked, preferred_element_type=F32) + bsp_ref[...]
        ya_ref[rows, :] = (u[rows] * mixed).astype(BF16)

    cos = cos_ref[...]
    sin = sin_ref[...]
    first_half = (lax.broadcasted_iota(jnp.int32, (tm, LANES), 1) % HEAD_DIM) < HEAD_DIM // 2

    def rope(pc):
        outs = []
        for p in range(C_WIDTH // LANES):
            xp = pc[:, p * LANES:(p + 1) * LANES]
            rot = jnp.where(first_half,
                            pltpu.roll(xp, LANES - HEAD_DIM // 2, axis=1),
                            pltpu.roll(xp, HEAD_DIM // 2, axis=1))
            outs.append(xp * cos + rot * sin)
        return jnp.concatenate(outs, axis=1)

    bq_ref[...] = (qk[:, :B_WIDTH] * (scale * LOG2_E)).astype(BF16)
    bk_ref[...] = qk[:, B_WIDTH:].astype(BF16)
    bv_ref[...] = vq[:, :B_WIDTH].astype(BF16)
    cq_ref[...] = rope(vq[:, B_WIDTH:]) * scale
    ck_ref[...] = rope(kv[:, :C_WIDTH])
    cv_ref[...] = kv[:, C_WIDTH:]


def _in_projection(x2d, mod_l, g_mix, w_in, cos, sin, g_sgu, w_sp, b_sp, *, layer, seq, tm):
    tokens, d = x2d.shape
    per_batch = seq // tm
    in_width = w_in.shape[-1]
    w_cat = jnp.transpose(w_sp, (1, 0, 2)).reshape(CHUNK, A_GROUPS * CHUNK)
    bias = jnp.repeat(b_sp.T, HEAD_DIM, axis=1)
    tile = lambda w: pl.BlockSpec((tm, w), lambda i: (i, 0))
    full = lambda a: pl.BlockSpec(a.shape, lambda i: (0,) * a.ndim)
    g_mix = g_mix.reshape(1, d)
    g_sgu = g_sgu.reshape(1, A_WIDTH)
    outs = pl.pallas_call(
        functools.partial(_inproj_kernel, tm=tm),
        grid=(tokens // tm,),
        in_specs=[tile(d),
                  pl.BlockSpec((1, N_MOD, d), lambda i: (i // per_batch, 0, 0)),
                  full(g_mix), _layer_weight(w_in, layer), tile(LANES), tile(LANES),
                  full(g_sgu), full(w_cat), full(bias)],
        out_specs=[tile(A_WIDTH)] + [tile(B_WIDTH)] * 3 + [tile(C_WIDTH)] * 3,
        out_shape=[jax.ShapeDtypeStruct((tokens, w), dt) for w, dt in
                   [(A_WIDTH, BF16)] + [(B_WIDTH, BF16)] * 3 + [(C_WIDTH, F32)] * 3],
        scratch_shapes=[pltpu.VMEM((tm, d), BF16)],
        compiler_params=_params("parallel"),
        name="in_projection",
    )(x2d, mod_l, g_mix, w_in, cos, sin, g_sgu, w_cat, bias)
    assert in_width == 2 * A_WIDTH + 3 * B_WIDTH + 3 * C_WIDTH
    return outs


def _stick_kernel(q_ref, k_ref, v_ref, later_ref, o_ref, acc_scr, carry_scr, top_scr, *, blk, lead,
                  n_q):
    first_q = pl.program_id(2) * n_q
    rows = HEADS_PER_TILE * blk
    lane_head = lax.broadcasted_iota(jnp.int32, (blk, LANES), 1) // HEAD_DIM
    t_idx = lax.broadcasted_iota(jnp.int32, (rows, blk), 0) % blk
    before = lax.broadcasted_iota(jnp.int32, (rows, blk), 1) < t_idx

    def stacked_q(j):
        q = q_ref[j * blk:(j + 1) * blk, :]
        return jnp.concatenate(
            [jnp.where(lane_head == h, q, jnp.zeros_like(q)) for h in range(HEADS_PER_TILE)], axis=0)

    def visit(q_stack, first_block, n_blocks, diagonal, carry, acc, out):
        start = pl.multiple_of(first_block * blk, blk)
        k_win = k_ref[pl.ds(start, n_blocks * blk), :]
        v_win = v_ref[pl.ds(start, n_blocks * blk), :]
        z = lax.dot_general(q_stack, k_win, (((1,), (1,)), ((), ())), preferred_element_type=F32)
        yield
        log_beta = jnp.minimum(z, 0.0) - jnp.log2(1.0 + jnp.exp2(-jnp.abs(z)))
        log_stay = log_beta - z
        parts = []
        for w in range(n_blocks):
            stay_w = log_stay[:, w * blk:(w + 1) * blk]
            if diagonal and w == n_blocks - 1:
                stay_w = jnp.where(before, stay_w, 0.0)
            parts.append(jnp.concatenate(_split_bf16(stay_w), axis=1))
        sums = jnp.dot(jnp.concatenate(parts, axis=0), later_ref[...], preferred_element_type=F32)
        yield
        weights = [None] * n_blocks
        for w in reversed(range(n_blocks)):
            sums_w = sums[w * rows:(w + 1) * rows]
            weight = jnp.exp2(log_beta[:, w * blk:(w + 1) * blk] + sums_w[:, :blk] + carry)
            if diagonal and w == n_blocks - 1:
                weight = jnp.where(before, weight, 0.0)
            weights[w] = weight.astype(BF16)
            carry = carry + sums_w[:, blk:]
        acc = acc + jnp.dot(jnp.concatenate(weights, axis=1), v_win, preferred_element_type=F32)
        out.append((carry, acc))

    def run_staged(visits):
        while visits:
            visits = [g for g in visits if next(g, True) is None]

    def first_visits(first_q, n_blocks):
        zero = jnp.zeros((rows, blk), F32)
        outs = [[] for _ in range(n_q)]
        run_staged([visit(stacked_q(j), first_q + j - (n_blocks[j] - 1), n_blocks[j], True,
                          zero, zero, outs[j]) for j in range(n_q)])
        for j in range(n_q):
            carry_scr[j], acc_scr[j] = outs[j][0]
        tops = [jnp.max(outs[j][0][0], axis=0, keepdims=True) for j in range(n_q)]
        top_scr[...] = jnp.broadcast_to(functools.reduce(jnp.maximum, tops), top_scr.shape)

    @pl.when(first_q == 0)
    def _():
        first_visits(0, [min(j + 1, lead) for j in range(n_q)])

    @pl.when(first_q > 0)
    def _():
        first_visits(first_q, [lead] * n_q)

    def alive(carry):
        top = jnp.max(carry, axis=0, keepdims=True)
        return (top[0, 0] > EXP2_ZERO_BELOW).astype(jnp.int32)

    def finish(j):
        def cond(state):
            kb, go = state
            return jnp.logical_and(kb >= 0, go > 0)

        def body(state):
            kb, _ = state
            out = []
            run_staged([visit(stacked_q(j), kb, 1, False, carry_scr[j], acc_scr[j], out)])
            carry_scr[j], acc_scr[j] = out[0]
            return kb - 1, alive(carry_scr[j])

        lax.while_loop(cond, body, (first_q + j - lead, alive(carry_scr[j])))

    @pl.when(alive(top_scr[...]) > 0)
    def _():
        for j in range(n_q):
            finish(j)

    for j in range(n_q):
        o_ref[j * blk:(j + 1) * blk, :] = jnp.where(
            lane_head == 0, acc_scr[j, :blk], acc_scr[j, blk:]).astype(BF16)


def _stick_breaking(q, k, v, *, bsz, seq):
    tokens, width = q.shape
    blk = Q_BLOCK
    n_q = STICK_Q_BLOCKS_PER_STEP
    assert n_q >= STICK_LEAD_BLOCKS - 1
    steps = seq // (blk * n_q)
    j_idx = lax.broadcasted_iota(jnp.int32, (2 * blk, 2 * blk), 0) % blk
    s_idx = lax.broadcasted_iota(jnp.int32, (2 * blk, 2 * blk), 1)
    later = jnp.where((j_idx > s_idx) | (s_idx >= blk), 1.0, 0.0).astype(BF16)
    q_spec = pl.BlockSpec((n_q * blk, LANES), lambda b, p, i: (b * steps + i, p))
    kv_spec = pl.BlockSpec((seq, LANES), lambda b, p, i: (b, p))
    return pl.pallas_call(
        functools.partial(_stick_kernel, blk=blk, lead=STICK_LEAD_BLOCKS, n_q=n_q),
        grid=(bsz, width // LANES, steps),
        in_specs=[q_spec, kv_spec, kv_spec, pl.BlockSpec(later.shape, lambda b, p, i: (0, 0))],
        out_specs=q_spec,
        out_shape=jax.ShapeDtypeStruct((tokens, width), BF16),
        scratch_shapes=[pltpu.VMEM((n_q, HEADS_PER_TILE * blk, LANES), F32),
                        pltpu.VMEM((n_q, HEADS_PER_TILE * blk, blk), F32),
                        pltpu.VMEM((8, blk), F32)],
        compiler_params=_params("parallel", "parallel", "parallel"),
        name="stick_breaking",
    )(q, k, v, later)


def _window_kernel(q_ref, k_ref, v_ref, o_ref, lse_ref, *, seq, max_tq):
    pair = pl.program_id(1)
    rows = HEADS_PER_TILE * WINDOW
    lane_head = lax.broadcasted_iota(jnp.int32, (WINDOW, LANES), 1) // HEAD_DIM
    off = (lax.broadcasted_iota(jnp.int32, (rows, 2 * WINDOW), 1)
           - lax.broadcasted_iota(jnp.int32, (rows, 2 * WINDOW), 0) % WINDOW)
    band = (off >= 0) & (off <= WINDOW)
    from_prev = lax.broadcasted_iota(jnp.int32, (rows, 2 * WINDOW), 1) < WINDOW

    def class_rows(first, d):
        return pl.ds(first, WINDOW) if d == 1 else pl.ds(first, WINDOW, stride=d)

    def block(d, r, j, prev, out):
        cur = class_rows(r + d * j, d)
        q = q_ref[cur, :].astype(BF16)
        k_cur = k_ref[cur, :].astype(BF16)
        v_cur = v_ref[cur, :].astype(BF16)
        out.append((k_cur, v_cur))
        if prev is None:
            before = class_rows(r + d * jnp.maximum(j - WINDOW, 0), d)
            prev = (k_ref[before, :].astype(BF16), v_ref[before, :].astype(BF16))
            valid = band & (jnp.logical_not(from_prev) | (j > 0))
        else:
            valid = band
        q_stack = jnp.concatenate(
            [jnp.where(lane_head == h, q, jnp.zeros_like(q)) for h in range(HEADS_PER_TILE)], axis=0)
        z = lax.dot_general(q_stack, jnp.concatenate([prev[0], k_cur], axis=0),
                            (((1,), (1,)), ((), ())), preferred_element_type=F32)
        yield
        z = jnp.where(valid, z, -jnp.inf)
        z_max = jnp.max(z, axis=-1, keepdims=True)
        prob = jnp.exp(z - z_max)
        denom = jnp.sum(prob, axis=-1, keepdims=True)
        pv = jnp.dot(prob.astype(BF16), jnp.concatenate([prev[1], v_cur], axis=0),
                     preferred_element_type=F32)
        yield
        o = pv * (1.0 / denom)
        lse = z_max + jnp.log(denom)
        o_ref[cur, :] = jnp.where(lane_head == 0, o[:WINDOW], o[WINDOW:])
        lse_ref[cur, :] = jnp.where(lane_head == 0, lse[:WINDOW], lse[WINDOW:])

    def tiles(d):
        tq = min(max_tq, seq // d)
        runs = max_tq // tq
        per_class = seq // d // tq

        def tile(i, _):
            visits = []
            for c in range(runs):
                run = i * runs + c
                r = run // per_class
                j0 = (run % per_class) * tq
                prev = None
                for s in range(tq // WINDOW):
                    out = []
                    visits.append(block(d, r, j0 + s * WINDOW, prev, out))
                    next(visits[-1])
                    prev = out[0]
            while visits:
                visits = [g for g in visits if next(g, True) is None]
            return 0

        lax.fori_loop(0, seq // max_tq, tile, 0)

    for idx, d in enumerate(DILATIONS):
        pl.when(pair == idx)(functools.partial(tiles, d))


def _dilated_window(q, k, v, *, bsz, seq):
    tokens, width = q.shape
    spec = pl.BlockSpec((seq, LANES), lambda b, p: (b, p))
    return pl.pallas_call(
        functools.partial(_window_kernel, seq=seq, max_tq=WINDOW_TILE_ROWS),
        grid=(bsz, width // LANES),
        in_specs=[spec] * 3,
        out_specs=[spec] * 2,
        out_shape=[jax.ShapeDtypeStruct((tokens, width), F32)] * 2,
        compiler_params=_params("parallel", "parallel"),
        name="dilated_window",
    )(q, k, v)


def _layer_tail_kernel(x_ref, mod_ref, ya_ref, yb_ref, oc_ref, lse_ref, wout_ref, g_ref, wup_ref,
                       cw_ref, cb_ref, wdn_ref, gfin_ref, o_ref,
                       y_scr, x_scr, tail_scr, h_scr, up_scr, act_scr, down_scr, *, tm, tn, per_batch,
                       final):
    i = pl.program_id(0)
    mod = mod_ref[0]
    d_ff = wdn_ref.shape[0]

    y_scr[:, :A_WIDTH] = ya_ref[...]
    y_scr[:, A_WIDTH:A_WIDTH + B_WIDTH] = yb_ref[...]
    pairs = C_WIDTH // LANES
    lse = [lse_ref[:, p * LANES:(p + 1) * LANES] for p in range(pairs)]
    top = functools.reduce(jnp.maximum, lse)
    share = [jnp.exp(l - top) for l in lse]
    inv = 1.0 / functools.reduce(lambda a, b: a + b, share)
    c0 = A_WIDTH + B_WIDTH
    for p in range(pairs):
        o_p = oc_ref[:, p * LANES:(p + 1) * LANES]
        y_scr[:, c0 + p * LANES:c0 + (p + 1) * LANES] = (o_p * (share[p] * inv)).astype(BF16)
    mixed = jnp.dot(y_scr[...], wout_ref[...], preferred_element_type=F32)
    x_scr[...] = x_ref[...] + (1.0 + mod[2:3]) * mixed

    def norm_mod(xv):
        return _rms_scale(xv) * g_ref[...] * (1.0 + mod[4:5]) + mod[3:4]

    opens_sequence = i % per_batch == 0

    @pl.when(opens_sequence)
    def _():
        tail_scr[...] = jnp.zeros_like(tail_scr)

    halo = jnp.where(opens_sequence, 0.0, norm_mod(tail_scr[...]))
    h_scr[:CONV_HALO, :] = halo.astype(BF16)
    h_scr[CONV_HALO:, :] = norm_mod(x_scr[...]).astype(BF16)
    tail_scr[...] = x_scr[tm - CONV_HALO:, :]

    half = tm // 2

    slabs = range(tn // LANES)
    n_tiles = d_ff // tn
    in_flight = up_scr.shape[0] // 2

    def project(n):
        for part in range(2):
            c0 = part * d_ff + n * tn
            up = jnp.dot(h_scr[...], wup_ref[:, c0:c0 + tn], preferred_element_type=F32)
            for s in slabs:
                up_scr[2 * (n % in_flight) + part, s] = up[:, s * LANES:(s + 1) * LANES]

    def conv(slot, c0):
        w = cw_ref[:, c0:c0 + tn]
        bias = cb_ref[:, c0:c0 + tn]
        parities = []
        for parity in range(2):
            cols = []
            for s in slabs:
                lanes = slice(s * LANES, (s + 1) * LANES)
                out = bias[:, lanes]
                for tap in range(CONV_WIDTH):
                    first = CONV_HALO - (CONV_WIDTH - 1) + tap + parity
                    out = out + w[tap:tap + 1, lanes] * up_scr[slot, s, pl.ds(first, half, stride=2), :]
                cols.append(out)
            parities.append(jnp.concatenate(cols, axis=1))
        return jnp.concatenate(parities, axis=0)

    project(0)
    for n in range(n_tiles):
        if n + 1 < n_tiles:
            project(n + 1)
        slot = 2 * (n % in_flight)
        gate = conv(slot, n * tn)
        val = conv(slot + 1, d_ff + n * tn)
        act = gate * (1.0 / (1.0 + jnp.exp(-gate))) * val
        act_scr[:, n * tn:(n + 1) * tn] = act.astype(BF16)
    down = jnp.dot(act_scr[...], wdn_ref[...], preferred_element_type=F32)
    for s in range(down_scr.shape[0]):
        lanes = slice(s * LANES, (s + 1) * LANES)
        down_scr[s, pl.ds(0, half, stride=2), :] = down[:half, lanes]
        down_scr[s, pl.ds(1, half, stride=2), :] = down[half:, lanes]
    down = jnp.concatenate([down_scr[s] for s in range(down_scr.shape[0])], axis=1)
    x_new = x_scr[...] + (1.0 + mod[5:6]) * down
    if final:
        x_new = _rms_scale(x_new) * gfin_ref[...]
    o_ref[...] = x_new


def _layer_tail(x2d, mod_l, y_a, y_b, o_c, lse, w_out, g_ffn, w_up, conv_w, conv_b, w_down, g_final,
                *, layer, seq, tm, tn, final):
    tokens, d = x2d.shape
    per_batch = seq // tm
    tile = lambda w: pl.BlockSpec((tm, w), lambda i: (i, 0))
    full = lambda a: pl.BlockSpec(a.shape, lambda i: (0,) * a.ndim)
    resident = lambda a: _layer_weight(a, layer)
    g_ffn = g_ffn.reshape(1, d)
    conv_b = conv_b.reshape(1, -1)
    g_final = g_final.reshape(1, d)
    return pl.pallas_call(
        functools.partial(_layer_tail_kernel, tm=tm, tn=tn, per_batch=per_batch, final=final),
        grid=(tokens // tm,),
        in_specs=[tile(d),
                  pl.BlockSpec((1, N_MOD, d), lambda i: (i // per_batch, 0, 0)),
                  tile(A_WIDTH), tile(B_WIDTH), tile(C_WIDTH), tile(C_WIDTH), resident(w_out),
                  full(g_ffn), resident(w_up), full(conv_w), full(conv_b), resident(w_down),
                  full(g_final)],
        out_specs=tile(d),
        out_shape=jax.ShapeDtypeStruct((tokens, d), F32),
        scratch_shapes=[pltpu.VMEM((tm, w_out.shape[1]), BF16),
                        pltpu.VMEM((tm, d), F32),
                        pltpu.VMEM((CONV_HALO, d), F32),
                        pltpu.VMEM((tm + CONV_HALO, d), BF16),
                        pltpu.VMEM((4, tn // LANES, tm + CONV_HALO, LANES), F32),
                        pltpu.VMEM((tm, w_down.shape[1]), BF16),
                        pltpu.VMEM((d // LANES, tm, LANES), F32)],
        compiler_params=_params("arbitrary"),
        name="layer_tail",
    )(x2d, mod_l, y_a, y_b, o_c, lse, w_out, g_ffn, w_up, conv_w, conv_b, w_down, g_final)


def kernel(x, c, positions, w_ada, b_ada, g_mix, w_in, g_sgu, w_sp, b_sp, w_out, g_ffn, w_up,
           conv_w, conv_b, w_down, g_final):
    bsz, seq, d = x.shape
    depth = w_ada.shape[0]
    tm = min(512, seq)
    assert seq % tm == 0 and seq % (max(DILATIONS) * WINDOW) == 0
    mod = _modulation(c, w_ada, b_ada)
    cos, sin = _rope_tables(positions)
    cos = cos.reshape(bsz * seq, LANES)
    sin = sin.reshape(bsz * seq, LANES)
    x2d = x.reshape(bsz * seq, d)
    w_in, w_out, w_up, w_down = (w.astype(BF16) for w in (w_in, w_out, w_up, w_down))
    for l in range(depth):
        y_a, b_q, b_k, b_v, c_q, c_k, c_v = _in_projection(
            x2d, mod[l], g_mix[l], w_in, cos, sin, g_sgu[l], w_sp[l], b_sp[l],
            layer=l, seq=seq, tm=tm)
        y_b = _stick_breaking(b_q, b_k, b_v, bsz=bsz, seq=seq)
        o_c, lse = _dilated_window(c_q, c_k, c_v, bsz=bsz, seq=seq)
        x2d = _layer_tail(x2d, mod[l], y_a, y_b, o_c, lse, w_out, g_ffn[l], w_up, conv_w[l],
                          conv_b[l], w_down, g_final, layer=l, seq=seq, tm=tm, tn=256,
                          final=(l == depth - 1))
    return x2d.reshape(bsz, seq, d)
```

```python
import functools

import jax
import jax.numpy as jnp
from jax import lax
from jax.experimental import pallas as pl
from jax.experimental.pallas import tpu as pltpu

F32 = jnp.float32
BF16 = jnp.bfloat16

LANES = 128
MXU_COLUMNS = 256
HEAD_DIM = 64
HEADS_PER_TILE = LANES // HEAD_DIM
A_GROUPS = 4
A_WIDTH = A_GROUPS * HEAD_DIM
CHUNK = 128
B_WIDTH = 6 * HEAD_DIM
C_WIDTH = 6 * HEAD_DIM
DILATIONS = (1, 4, 16)
WINDOW = 128
WINDOW_TILE_ROWS = 512
Q_BLOCK = 128
STICK_LEAD_BLOCKS = 3
STICK_Q_BLOCKS_PER_STEP = 16
CONV_WIDTH = 3
CONV_HALO = 8
ROPE_THETA = 10000.0
EPS = 1e-6
N_MOD = 6
LOG2_E = 1.4426950408889634
EXP2_ZERO_BELOW = -151.0
VMEM_LIMIT_BYTES = 56 * 1024 * 1024


def _params(*semantics):
    return pltpu.CompilerParams(dimension_semantics=semantics,
                                vmem_limit_bytes=VMEM_LIMIT_BYTES)


def _layer_weight(stacked, layer):
    return pl.BlockSpec((None,) + stacked.shape[1:], lambda i: (layer, 0, 0),
                        pipeline_mode=pl.Buffered(1))


def _split_bf16(a):
    hi = a.astype(BF16)
    lo = (a - hi.astype(F32)).astype(BF16)
    return hi, lo


def _rms_scale(x):
    return x * lax.rsqrt(jnp.mean(x * x, axis=-1, keepdims=True) + EPS)


def _mod_kernel(c_ref, w_ref, b_ref, o_ref):
    c = c_ref[...]
    c_act = c * (1.0 / (1.0 + jnp.exp(-c)))
    o_ref[0] = jnp.dot(c_act, w_ref[0], preferred_element_type=F32) + b_ref[0]


def _modulation(c, w_ada, b_ada):
    depth, d, nd = w_ada.shape
    bsz = c.shape[0]
    rows = -(-bsz // 8) * 8
    c_pad = jnp.zeros((rows, d), F32).at[:bsz].set(c)
    out = pl.pallas_call(
        _mod_kernel,
        grid=(depth, nd // d),
        in_specs=[pl.BlockSpec((rows, d), lambda l, j: (0, 0)),
                  pl.BlockSpec((1, d, d), lambda l, j: (l, 0, j)),
                  pl.BlockSpec((1, 1, d), lambda l, j: (l, 0, j))],
        out_specs=pl.BlockSpec((1, rows, d), lambda l, j: (l, 0, j)),
        out_shape=jax.ShapeDtypeStruct((depth, rows, nd), F32),
        compiler_params=_params("parallel", "parallel"),
        name="adaln_modulation",
    )(c_pad, w_ada, b_ada.reshape(depth, 1, nd))
    return out[:, :bsz].reshape(depth, bsz, N_MOD, d)


def _rope_kernel(pos_ref, freq_ref, sign_ref, cos_ref, sin_ref):
    ang = pos_ref[0] * freq_ref[...]
    cos_ref[0] = jnp.cos(ang)
    sin_ref[0] = jnp.sin(ang) * sign_ref[...]


def _rope_tables(positions):
    bsz, seq = positions.shape
    half = HEAD_DIM // 2
    inv_freq = ROPE_THETA ** (-jnp.arange(0, HEAD_DIM, 2, dtype=F32) / HEAD_DIM)
    freq = jnp.tile(inv_freq, LANES // half).reshape(1, LANES)
    sign = jnp.tile(jnp.concatenate([-jnp.ones((half,), F32), jnp.ones((half,), F32)]),
                    HEADS_PER_TILE).reshape(1, LANES)
    pos = positions.astype(F32).reshape(bsz, seq, 1)
    row = pl.BlockSpec((1, LANES), lambda b: (0, 0))
    tab = pl.BlockSpec((1, seq, LANES), lambda b: (b, 0, 0))
    return pl.pallas_call(
        _rope_kernel,
        grid=(bsz,),
        in_specs=[pl.BlockSpec((1, seq, 1), lambda b: (b, 0, 0)), row, row],
        out_specs=[tab, tab],
        out_shape=[jax.ShapeDtypeStruct((bsz, seq, LANES), F32)] * 2,
        compiler_params=_params("parallel"),
        name="rope_tables",
    )(pos, freq, sign)


def _inproj_kernel(x_ref, mod_ref, g_ref, w_ref, cos_ref, sin_ref, gsgu_ref, wsp_ref, bsp_ref,
                   ya_ref, bq_ref, bk_ref, bv_ref, cq_ref, ck_ref, cv_ref, h_scr, *, tm):
    mod = mod_ref[0]
    h = _rms_scale(x_ref[...]) * g_ref[...] * (1.0 + mod[1:2]) + mod[0:1]
    h_scr[...] = h.astype(BF16)
    scale = HEAD_DIM ** -0.5

    def proj(c0, width):
        return jnp.dot(h_scr[...], w_ref[:, c0:c0 + width], preferred_element_type=F32)

    assert B_WIDTH == C_WIDTH and (2 * B_WIDTH) % MXU_COLUMNS == 0
    pa = proj(0, 2 * A_WIDTH)
    c0 = 2 * A_WIDTH
    qk = proj(c0, 2 * B_WIDTH)
    vq = proj(c0 + 2 * B_WIDTH, 2 * B_WIDTH)
    kv = proj(c0 + 4 * B_WIDTH, 2 * B_WIDTH)
    u = jax.nn.gelu(pa[:, :A_WIDTH])
    v = jax.nn.gelu(pa[:, A_WIDTH:])
    grp_r = lax.broadcasted_iota(jnp.int32, (A_WIDTH, A_WIDTH), 0) // HEAD_DIM
    grp_c = lax.broadcasted_iota(jnp.int32, (A_WIDTH, A_WIDTH), 1) // HEAD_DIM
    group_ones = jnp.where(grp_r == grp_c, 1.0, 0.0).astype(BF16)
    hi, lo = _split_bf16(v * v)
    ssq = (jnp.dot(hi, group_ones, preferred_element_type=F32)
           + jnp.dot(lo, group_ones, preferred_element_type=F32))
    vn = v * lax.rsqrt(ssq * (1.0 / HEAD_DIM) + EPS) * gsgu_ref[...]
    t_idx = lax.broadcasted_iota(jnp.int32, (CHUNK, A_GROUPS * CHUNK), 0)
    s_idx = lax.broadcasted_iota(jnp.int32, (CHUNK, A_GROUPS * CHUNK), 1) % CHUNK
    w_causal = jnp.where(s_idx <= t_idx, wsp_ref[...], 0.0).astype(BF16)
    lane_grp = lax.broadcasted_iota(jnp.int32, (CHUNK, A_WIDTH), 1) // HEAD_DIM
    for c in range(tm // CHUNK):
        rows = slice(c * CHUNK, (c + 1) * CHUNK)
        vc = vn[rows]
        stacked = jnp.concatenate(
            [jnp.where(lane_grp == g, vc, 0.0) for g in range(A_GROUPS)], axis=0).astype(BF16)
        mixed = jnp.dot(w_causal, stacked, preferred_element_type=F32) + bsp_ref[...]
        ya_ref[rows, :] = (u[rows] * mixed).astype(BF16)

    cos = cos_ref[...]
    sin = sin_ref[...]
    first_half = (lax.broadcasted_iota(jnp.int32, (tm, LANES), 1) % HEAD_DIM) < HEAD_DIM // 2

    def rope(pc):
        outs = []
        for p in range(C_WIDTH // LANES):
            xp = pc[:, p * LANES:(p + 1) * LANES]
            rot = jnp.where(first_half,
                            pltpu.roll(xp, LANES - HEAD_DIM // 2, axis=1),
                            pltpu.roll(xp, HEAD_DIM // 2, axis=1))
            outs.append(xp * cos + rot * sin)
        return jnp.concatenate(outs, axis=1)

    bq_ref[...] = (qk[:, :B_WIDTH] * (scale * LOG2_E)).astype(BF16)
    bk_ref[...] = qk[:, B_WIDTH:].astype(BF16)
    bv_ref[...] = vq[:, :B_WIDTH].astype(BF16)
    cq_ref[...] = rope(vq[:, B_WIDTH:]) * scale
    ck_ref[...] = rope(kv[:, :C_WIDTH])
    cv_ref[...] = kv[:, C_WIDTH:]


def _in_projection(x2d, mod_l, g_mix, w_in, cos, sin, g_sgu, w_sp, b_sp, *, layer, seq, tm):
    tokens, d = x2d.shape
    per_batch = seq // tm
    in_width = w_in.shape[-1]
    w_cat = jnp.transpose(w_sp, (1, 0, 2)).reshape(CHUNK, A_GROUPS * CHUNK)
    bias = jnp.repeat(b_sp.T, HEAD_DIM, axis=1)
    tile = lambda w: pl.BlockSpec((tm, w), lambda i: (i, 0))
    full = lambda a: pl.BlockSpec(a.shape, lambda i: (0,) * a.ndim)
    g_mix = g_mix.reshape(1, d)
    g_sgu = g_sgu.reshape(1, A_WIDTH)
    outs = pl.pallas_call(
        functools.partial(_inproj_kernel, tm=tm),
        grid=(tokens // tm,),
        in_specs=[tile(d),
                  pl.BlockSpec((1, N_MOD, d), lambda i: (i // per_batch, 0, 0)),
                  full(g_mix), _layer_weight(w_in, layer), tile(LANES), tile(LANES),
                  full(g_sgu), full(w_cat), full(bias)],
        out_specs=[tile(A_WIDTH)] + [tile(B_WIDTH)] * 3 + [tile(C_WIDTH)] * 3,
        out_shape=[jax.ShapeDtypeStruct((tokens, w), dt) for w, dt in
                   [(A_WIDTH, BF16)] + [(B_WIDTH, BF16)] * 3 + [(C_WIDTH, F32)] * 3],
        scratch_shapes=[pltpu.VMEM((tm, d), BF16)],
        compiler_params=_params("parallel"),
        name="in_projection",
    )(x2d, mod_l, g_mix, w_in, cos, sin, g_sgu, w_cat, bias)
    assert in_width == 2 * A_WIDTH + 3 * B_WIDTH + 3 * C_WIDTH
    return outs


def _stick_kernel(q_ref, k_ref, v_ref, later_ref, o_ref, acc_scr, carry_scr, top_scr, *, blk, lead,
                  n_q):
    first_q = pl.program_id(2) * n_q
    rows = HEADS_PER_TILE * blk
    lane_head = lax.broadcasted_iota(jnp.int32, (blk, LANES), 1) // HEAD_DIM
    t_idx = lax.broadcasted_iota(jnp.int32, (rows, blk), 0) % blk
    before = lax.broadcasted_iota(jnp.int32, (rows, blk), 1) < t_idx

    def stacked_q(j):
        q = q_ref[j * blk:(j + 1) * blk, :]
        return jnp.concatenate(
            [jnp.where(lane_head == h, q, jnp.zeros_like(q)) for h in range(HEADS_PER_TILE)], axis=0)

    half = blk // 2

    def top_rows(a):
        return jnp.concatenate([a[h * blk:h * blk + half] for h in range(HEADS_PER_TILE)], axis=0)

    def with_top_rows(a, top):
        pieces = []
        for h in range(HEADS_PER_TILE):
            pieces += [top[h * half:(h + 1) * half], a[h * blk + half:(h + 1) * blk]]
        return jnp.concatenate(pieces, axis=0)

    def scores(q_rows, first_block, n_blocks):
        start = pl.multiple_of(first_block * blk, blk)
        k_win = k_ref[pl.ds(start, n_blocks * blk), :]
        z = lax.dot_general(q_rows, k_win, (((1,), (1,)), ((), ())), preferred_element_type=F32)
        return z, v_ref[pl.ds(start, n_blocks * blk), :]

    def log_terms(z):
        log_beta = jnp.minimum(z, 0.0) - jnp.log2(1.0 + jnp.exp2(-jnp.abs(z)))
        return log_beta, log_beta - z

    def visit(q_stack, first_block, n_blocks, diagonal, carry, acc, out, oldest_top_only=False):
        skip = 1 if oldest_top_only else 0
        n_full = n_blocks - skip
        z, v_win = scores(q_stack, first_block + skip, n_full)
        if oldest_top_only:
            z_old, v_old = scores(top_rows(q_stack), first_block, 1)
        yield
        log_beta, log_stay = log_terms(z)
        parts = []
        for w in range(n_full):
            stay_w = log_stay[:, w * blk:(w + 1) * blk]
            if diagonal and w == n_full - 1:
                stay_w = jnp.where(before, stay_w, 0.0)
            parts.append(jnp.concatenate(_split_bf16(stay_w), axis=1))
        if oldest_top_only:
            log_beta_old, log_stay_old = log_terms(z_old)
            parts.append(jnp.concatenate(_split_bf16(log_stay_old), axis=1))
        sums = jnp.dot(jnp.concatenate(parts, axis=0), later_ref[...], preferred_element_type=F32)
        yield
        weights = [None] * n_full
        for w in reversed(range(n_full)):
            sums_w = sums[w * rows:(w + 1) * rows]
            weight = jnp.exp2(log_beta[:, w * blk:(w + 1) * blk] + sums_w[:, :blk] + carry)
            if diagonal and w == n_full - 1:
                weight = jnp.where(before, weight, 0.0)
            weights[w] = weight.astype(BF16)
            carry = carry + sums_w[:, blk:]
        acc = acc + jnp.dot(jnp.concatenate(weights, axis=1), v_win, preferred_element_type=F32)
        if oldest_top_only:
            sums_old = sums[n_full * rows:]
            carry_top = top_rows(carry)
            weight = jnp.exp2(log_beta_old + sums_old[:, :blk] + carry_top)
            acc_top = top_rows(acc) + jnp.dot(weight.astype(BF16), v_old, preferred_element_type=F32)
            carry = with_top_rows(carry, carry_top + sums_old[:, blk:])
            acc = with_top_rows(acc, acc_top)
        out.append((carry, acc))

    def run_staged(visits):
        while visits:
            visits = [g for g in visits if next(g, True) is None]

    zero = jnp.zeros((rows, blk), F32)

    def first_visits(first_q, n_blocks):
        outs = [[] for _ in range(n_q)]
        run_staged([visit(stacked_q(j), first_q + j - (n_blocks[j] - 1), n_blocks[j], True,
                          zero, zero, outs[j], oldest_top_only=n_blocks[j] == lead)
                    for j in range(n_q)])
        for j in range(n_q):
            carry_scr[j], acc_scr[j] = outs[j][0]
        tops = [jnp.max(outs[j][0][0], axis=0, keepdims=True)
                for j in range(n_q) if n_blocks[j] == lead]
        top_scr[...] = jnp.broadcast_to(functools.reduce(jnp.maximum, tops), top_scr.shape)

    @pl.when(first_q == 0)
    def _():
        first_visits(0, [min(j + 1, lead) for j in range(n_q)])

    @pl.when(first_q > 0)
    def _():
        first_visits(first_q, [lead] * n_q)

    def alive(carry):
        top = jnp.max(carry, axis=0, keepdims=True)
        return (top[0, 0] > EXP2_ZERO_BELOW).astype(jnp.int32)

    def redo(j):
        out = []
        run_staged([visit(stacked_q(j), first_q + j, 1, True, zero, zero, out)])
        carry_scr[j], acc_scr[j] = out[0]

        def cond(state):
            kb, go = state
            return jnp.logical_and(kb >= 0, go > 0)

        def body(state):
            kb, _ = state
            out = []
            run_staged([visit(stacked_q(j), kb, 1, False, carry_scr[j], acc_scr[j], out)])
            carry_scr[j], acc_scr[j] = out[0]
            return kb - 1, alive(carry_scr[j])

        lax.while_loop(cond, body, (first_q + j - 1, alive(carry_scr[j])))

    @pl.when(alive(top_scr[...]) > 0)
    def _():
        for j in range(n_q):
            redo(j)

    for j in range(n_q):
        o_ref[j * blk:(j + 1) * blk, :] = jnp.where(
            lane_head == 0, acc_scr[j, :blk], acc_scr[j, blk:]).astype(BF16)


def _stick_breaking(q, k, v, *, bsz, seq):
    tokens, width = q.shape
    blk = Q_BLOCK
    n_q = STICK_Q_BLOCKS_PER_STEP
    assert n_q >= STICK_LEAD_BLOCKS
    steps = seq // (blk * n_q)
    j_idx = lax.broadcasted_iota(jnp.int32, (2 * blk, 2 * blk), 0) % blk
    s_idx = lax.broadcasted_iota(jnp.int32, (2 * blk, 2 * blk), 1)
    later = jnp.where((j_idx > s_idx) | (s_idx >= blk), 1.0, 0.0).astype(BF16)
    q_spec = pl.BlockSpec((n_q * blk, LANES), lambda b, p, i: (b * steps + i, p))
    kv_spec = pl.BlockSpec((seq, LANES), lambda b, p, i: (b, p))
    return pl.pallas_call(
        functools.partial(_stick_kernel, blk=blk, lead=STICK_LEAD_BLOCKS, n_q=n_q),
        grid=(bsz, width // LANES, steps),
        in_specs=[q_spec, kv_spec, kv_spec, pl.BlockSpec(later.shape, lambda b, p, i: (0, 0))],
        out_specs=q_spec,
        out_shape=jax.ShapeDtypeStruct((tokens, width), BF16),
        scratch_shapes=[pltpu.VMEM((n_q, HEADS_PER_TILE * blk, LANES), F32),
                        pltpu.VMEM((n_q, HEADS_PER_TILE * blk, blk), F32),
                        pltpu.VMEM((8, blk), F32)],
        compiler_params=_params("parallel", "parallel", "parallel"),
        name="stick_breaking",
    )(q, k, v, later)


def _window_kernel(q_ref, k_ref, v_ref, o_ref, lse_ref, *, seq, max_tq):
    pair = pl.program_id(1)
    rows = HEADS_PER_TILE * WINDOW
    lane_head = lax.broadcasted_iota(jnp.int32, (WINDOW, LANES), 1) // HEAD_DIM
    off = (lax.broadcasted_iota(jnp.int32, (rows, 2 * WINDOW), 1)
           - lax.broadcasted_iota(jnp.int32, (rows, 2 * WINDOW), 0) % WINDOW)
    band = (off >= 0) & (off <= WINDOW)
    from_prev = lax.broadcasted_iota(jnp.int32, (rows, 2 * WINDOW), 1) < WINDOW

    def class_rows(first, d):
        return pl.ds(first, WINDOW) if d == 1 else pl.ds(first, WINDOW, stride=d)

    def block(d, r, j, prev, out):
        cur = class_rows(r + d * j, d)
        q = q_ref[cur, :].astype(BF16)
        k_cur = k_ref[cur, :].astype(BF16)
        v_cur = v_ref[cur, :].astype(BF16)
        out.append((k_cur, v_cur))
        if prev is None:
            before = class_rows(r + d * jnp.maximum(j - WINDOW, 0), d)
            prev = (k_ref[before, :].astype(BF16), v_ref[before, :].astype(BF16))
            valid = band & (jnp.logical_not(from_prev) | (j > 0))
        else:
            valid = band
        q_stack = jnp.concatenate(
            [jnp.where(lane_head == h, q, jnp.zeros_like(q)) for h in range(HEADS_PER_TILE)], axis=0)
        z = lax.dot_general(q_stack, jnp.concatenate([prev[0], k_cur], axis=0),
                            (((1,), (1,)), ((), ())), preferred_element_type=F32)
        yield
        z = jnp.where(valid, z, -jnp.inf)
        z_max = jnp.max(z, axis=-1, keepdims=True)
        prob = jnp.exp(z - z_max)
        denom = jnp.sum(prob, axis=-1, keepdims=True)
        pv = jnp.dot(prob.astype(BF16), jnp.concatenate([prev[1], v_cur], axis=0),
                     preferred_element_type=F32)
        yield
        o = pv * (1.0 / denom)
        lse = z_max + jnp.log(denom)
        o_ref[cur, :] = jnp.where(lane_head == 0, o[:WINDOW], o[WINDOW:])
        lse_ref[cur, :] = jnp.where(lane_head == 0, lse[:WINDOW], lse[WINDOW:])

    def tiles(d):
        tq = min(max_tq, seq // d)
        runs = max_tq // tq
        per_class = seq // d // tq

        def tile(i, _):
            visits = []
            for c in range(runs):
                run = i * runs + c
                r = run // per_class
                j0 = (run % per_class) * tq
                prev = None
                for s in range(tq // WINDOW):
                    out = []
                    visits.append(block(d, r, j0 + s * WINDOW, prev, out))
                    next(visits[-1])
                    prev = out[0]
            while visits:
                visits = [g for g in visits if next(g, True) is None]
            return 0

        lax.fori_loop(0, seq // max_tq, tile, 0)

    for idx, d in enumerate(DILATIONS):
        pl.when(pair == idx)(functools.partial(tiles, d))


def _dilated_window(q, k, v, *, bsz, seq):
    tokens, width = q.shape
    spec = pl.BlockSpec((seq, LANES), lambda b, p: (b, p))
    return pl.pallas_call(
        functools.partial(_window_kernel, seq=seq, max_tq=WINDOW_TILE_ROWS),
        grid=(bsz, width // LANES),
        in_specs=[spec] * 3,
        out_specs=[spec] * 2,
        out_shape=[jax.ShapeDtypeStruct((tokens, width), F32)] * 2,
        compiler_params=_params("parallel", "parallel"),
        name="dilated_window",
    )(q, k, v)


def _layer_tail_kernel(x_ref, mod_ref, ya_ref, yb_ref, oc_ref, lse_ref, wout_ref, g_ref, wup_ref,
                       cw_ref, cb_ref, wdn_ref, gfin_ref, o_ref,
                       y_scr, x_scr, tail_scr, h_scr, up_scr, act_scr, down_scr, *, tm, tn, per_batch,
                       final):
    i = pl.program_id(0)
    mod = mod_ref[0]
    d_ff = wdn_ref.shape[0]

    y_scr[:, :A_WIDTH] = ya_ref[...]
    y_scr[:, A_WIDTH:A_WIDTH + B_WIDTH] = yb_ref[...]
    pairs = C_WIDTH // LANES
    lse = [lse_ref[:, p * LANES:(p + 1) * LANES] for p in range(pairs)]
    top = functools.reduce(jnp.maximum, lse)
    share = [jnp.exp(l - top) for l in lse]
    inv = 1.0 / functools.reduce(lambda a, b: a + b, share)
    c0 = A_WIDTH + B_WIDTH
    for p in range(pairs):
        o_p = oc_ref[:, p * LANES:(p + 1) * LANES]
        y_scr[:, c0 + p * LANES:c0 + (p + 1) * LANES] = (o_p * (share[p] * inv)).astype(BF16)
    mixed = jnp.dot(y_scr[...], wout_ref[...], preferred_element_type=F32)
    x_scr[...] = x_ref[...] + (1.0 + mod[2:3]) * mixed

    def norm_mod(xv):
        return _rms_scale(xv) * g_ref[...] * (1.0 + mod[4:5]) + mod[3:4]

    opens_sequence = i % per_batch == 0

    @pl.when(opens_sequence)
    def _():
        tail_scr[...] = jnp.zeros_like(tail_scr)

    halo = jnp.where(opens_sequence, 0.0, norm_mod(tail_scr[...]))
    h_scr[:CONV_HALO, :] = halo.astype(BF16)
    h_scr[CONV_HALO:, :] = norm_mod(x_scr[...]).astype(BF16)
    tail_scr[...] = x_scr[tm - CONV_HALO:, :]

    half = tm // 2

    slabs = range(tn // LANES)
    n_tiles = d_ff // tn
    in_flight = up_scr.shape[0] // 2

    def project(n):
        for part in range(2):
            c0 = part * d_ff + n * tn
            up = jnp.dot(h_scr[...], wup_ref[:, c0:c0 + tn], preferred_element_type=F32)
            for s in slabs:
                up_scr[2 * (n % in_flight) + part, s] = up[:, s * LANES:(s + 1) * LANES]

    def conv(slot, c0):
        w = cw_ref[:, c0:c0 + tn]
        bias = cb_ref[:, c0:c0 + tn]
        parities = []
        for parity in range(2):
            cols = []
            for s in slabs:
                lanes = slice(s * LANES, (s + 1) * LANES)
                out = bias[:, lanes]
                for tap in range(CONV_WIDTH):
                    first = CONV_HALO - (CONV_WIDTH - 1) + tap + parity
                    out = out + w[tap:tap + 1, lanes] * up_scr[slot, s, pl.ds(first, half, stride=2), :]
                cols.append(out)
            parities.append(jnp.concatenate(cols, axis=1))
        return jnp.concatenate(parities, axis=0)

    project(0)
    for n in range(n_tiles):
        if n + 1 < n_tiles:
            project(n + 1)
        slot = 2 * (n % in_flight)
        gate = conv(slot, n * tn)
        val = conv(slot + 1, d_ff + n * tn)
        act = gate * (1.0 / (1.0 + jnp.exp(-gate))) * val
        act_scr[:, n * tn:(n + 1) * tn] = act.astype(BF16)
    down = jnp.dot(act_scr[...], wdn_ref[...], preferred_element_type=F32)
    for s in range(down_scr.shape[0]):
        lanes = slice(s * LANES, (s + 1) * LANES)
        down_scr[s, pl.ds(0, half, stride=2), :] = down[:half, lanes]
        down_scr[s, pl.ds(1, half, stride=2), :] = down[half:, lanes]
    down = jnp.concatenate([down_scr[s] for s in range(down_scr.shape[0])], axis=1)
    x_new = x_scr[...] + (1.0 + mod[5:6]) * down
    if final:
        x_new = _rms_scale(x_new) * gfin_ref[...]
    o_ref[...] = x_new


def _layer_tail(x2d, mod_l, y_a, y_b, o_c, lse, w_out, g_ffn, w_up, conv_w, conv_b, w_down, g_final,
                *, layer, seq, tm, tn, final):
    tokens, d = x2d.shape
    per_batch = seq // tm
    tile = lambda w: pl.BlockSpec((tm, w), lambda i: (i, 0))
    full = lambda a: pl.BlockSpec(a.shape, lambda i: (0,) * a.ndim)
    resident = lambda a: _layer_weight(a, layer)
    g_ffn = g_ffn.reshape(1, d)
    conv_b = conv_b.reshape(1, -1)
    g_final = g_final.reshape(1, d)
    return pl.pallas_call(
        functools.partial(_layer_tail_kernel, tm=tm, tn=tn, per_batch=per_batch, final=final),
        grid=(tokens // tm,),
        in_specs=[tile(d),
                  pl.BlockSpec((1, N_MOD, d), lambda i: (i // per_batch, 0, 0)),
                  tile(A_WIDTH), tile(B_WIDTH), tile(C_WIDTH), tile(C_WIDTH), resident(w_out),
                  full(g_ffn), resident(w_up), full(conv_w), full(conv_b), resident(w_down),
                  full(g_final)],
        out_specs=tile(d),
        out_shape=jax.ShapeDtypeStruct((tokens, d), F32),
        scratch_shapes=[pltpu.VMEM((tm, w_out.shape[1]), BF16),
                        pltpu.VMEM((tm, d), F32),
                        pltpu.VMEM((CONV_HALO, d), F32),
                        pltpu.VMEM((tm + CONV_HALO, d), BF16),
                        pltpu.VMEM((4, tn // LANES, tm + CONV_HALO, LANES), F32),
                        pltpu.VMEM((tm, w_down.shape[1]), BF16),
                        pltpu.VMEM((d // LANES, tm, LANES), F32)],
        compiler_params=_params("arbitrary"),
        name="layer_tail",
    )(x2d, mod_l, y_a, y_b, o_c, lse, w_out, g_ffn, w_up, conv_w, conv_b, w_down, g_final)


def kernel(x, c, positions, w_ada, b_ada, g_mix, w_in, g_sgu, w_sp, b_sp, w_out, g_ffn, w_up,
           conv_w, conv_b, w_down, g_final):
    bsz, seq, d = x.shape
    depth = w_ada.shape[0]
    tm = min(512, seq)
    assert seq % tm == 0 and seq % (max(DILATIONS) * WINDOW) == 0
    mod = _modulation(c, w_ada, b_ada)
    cos, sin = _rope_tables(positions)
    cos = cos.reshape(bsz * seq, LANES)
    sin = sin.reshape(bsz * seq, LANES)
    x2d = x.reshape(bsz * seq, d)
    w_in, w_out, w_up, w_down = (w.astype(BF16) for w in (w_in, w_out, w_up, w_down))
    for l in range(depth):
        y_a, b_q, b_k, b_v, c_q, c_k, c_v = _in_projection(
            x2d, mod[l], g_mix[l], w_in, cos, sin, g_sgu[l], w_sp[l], b_sp[l],
            layer=l, seq=seq, tm=tm)
        y_b = _stick_breaking(b_q, b_k, b_v, bsz=bsz, seq=seq)
        o_c, lse = _dilated_window(c_q, c_k, c_v, bsz=bsz, seq=seq)
        x2d = _layer_tail(x2d, mod[l], y_a, y_b, o_c, lse, w_out, g_ffn[l], w_up, conv_w[l],
                          conv_b[l], w_down, g_final, layer=l, seq=seq, tm=tm, tn=256,
                          final=(l == depth - 1))
    return x2d.reshape(bsz, seq, d)
```

```python
import functools

import jax
import jax.numpy as jnp
from jax import lax
from jax.experimental import pallas as pl
from jax.experimental.pallas import tpu as pltpu

F32 = jnp.float32
BF16 = jnp.bfloat16

LANES = 128
MXU_COLUMNS = 256
HEAD_DIM = 64
HEADS_PER_TILE = LANES // HEAD_DIM
A_GROUPS = 4
A_WIDTH = A_GROUPS * HEAD_DIM
CHUNK = 128
B_WIDTH = 6 * HEAD_DIM
C_WIDTH = 6 * HEAD_DIM
DILATIONS = (1, 4, 16)
WINDOW = 128
WINDOW_TILE_ROWS = 512
Q_BLOCK = 128
STICK_LEAD_BLOCKS = 3
STICK_Q_BLOCKS_PER_STEP = 16
CONV_WIDTH = 3
CONV_HALO = 8
ROPE_THETA = 10000.0
EPS = 1e-6
N_MOD = 6
LOG2_E = 1.4426950408889634
EXP2_ZERO_BELOW = -151.0
VMEM_LIMIT_BYTES = 56 * 1024 * 1024


def _params(*semantics):
    return pltpu.CompilerParams(dimension_semantics=semantics,
                                vmem_limit_bytes=VMEM_LIMIT_BYTES)


def _layer_weight(stacked, layer):
    return pl.BlockSpec((None,) + stacked.shape[1:], lambda i: (layer, 0, 0),
                        pipeline_mode=pl.Buffered(1))


def _split_bf16(a):
    hi = a.astype(BF16)
    lo = (a - hi.astype(F32)).astype(BF16)
    return hi, lo


def _rms_scale(x):
    return x * lax.rsqrt(jnp.mean(x * x, axis=-1, keepdims=True) + EPS)


def _mod_kernel(c_ref, w_ref, b_ref, o_ref):
    c = c_ref[...]
    c_act = c * (1.0 / (1.0 + jnp.exp(-c)))
    o_ref[0] = jnp.dot(c_act, w_ref[0], preferred_element_type=F32) + b_ref[0]


def _modulation(c, w_ada, b_ada):
    depth, d, nd = w_ada.shape
    bsz = c.shape[0]
    rows = -(-bsz // 8) * 8
    c_pad = jnp.zeros((rows, d), F32).at[:bsz].set(c)
    out = pl.pallas_call(
        _mod_kernel,
        grid=(depth, nd // d),
        in_specs=[pl.BlockSpec((rows, d), lambda l, j: (0, 0)),
                  pl.BlockSpec((1, d, d), lambda l, j: (l, 0, j)),
                  pl.BlockSpec((1, 1, d), lambda l, j: (l, 0, j))],
        out_specs=pl.BlockSpec((1, rows, d), lambda l, j: (l, 0, j)),
        out_shape=jax.ShapeDtypeStruct((depth, rows, nd), F32),
        compiler_params=_params("parallel", "parallel"),
        name="adaln_modulation",
    )(c_pad, w_ada, b_ada.reshape(depth, 1, nd))
    return out[:, :bsz].reshape(depth, bsz, N_MOD, d)


def _rope_kernel(pos_ref, freq_ref, sign_ref, cos_ref, sin_ref):
    ang = pos_ref[0] * freq_ref[...]
    cos_ref[0] = jnp.cos(ang)
    sin_ref[0] = jnp.sin(ang) * sign_ref[...]


def _rope_tables(positions):
    bsz, seq = positions.shape
    half = HEAD_DIM // 2
    inv_freq = ROPE_THETA ** (-jnp.arange(0, HEAD_DIM, 2, dtype=F32) / HEAD_DIM)
    freq = jnp.tile(inv_freq, LANES // half).reshape(1, LANES)
    sign = jnp.tile(jnp.concatenate([-jnp.ones((half,), F32), jnp.ones((half,), F32)]),
                    HEADS_PER_TILE).reshape(1, LANES)
    pos = positions.astype(F32).reshape(bsz, seq, 1)
    row = pl.BlockSpec((1, LANES), lambda b: (0, 0))
    tab = pl.BlockSpec((1, seq, LANES), lambda b: (b, 0, 0))
    return pl.pallas_call(
        _rope_kernel,
        grid=(bsz,),
        in_specs=[pl.BlockSpec((1, seq, 1), lambda b: (b, 0, 0)), row, row],
        out_specs=[tab, tab],
        out_shape=[jax.ShapeDtypeStruct((bsz, seq, LANES), F32)] * 2,
        compiler_params=_params("parallel"),
        name="rope_tables",
    )(pos, freq, sign)


def _inproj_kernel(x_ref, mod_ref, g_ref, w_ref, cos_ref, sin_ref, gsgu_ref, wsp_ref, bsp_ref,
                   ya_ref, bq_ref, bk_ref, bv_ref, cq_ref, ck_ref, cv_ref, h_scr, *, tm):
    mod = mod_ref[0]
    h = _rms_scale(x_ref[...]) * g_ref[...] * (1.0 + mod[1:2]) + mod[0:1]
    h_scr[...] = h.astype(BF16)
    scale = HEAD_DIM ** -0.5

    def proj(c0, width):
        return jnp.dot(h_scr[...], w_ref[:, c0:c0 + width], preferred_element_type=F32)

    assert B_WIDTH == C_WIDTH and (2 * B_WIDTH) % MXU_COLUMNS == 0
    pa = proj(0, 2 * A_WIDTH)
    c0 = 2 * A_WIDTH
    qk = proj(c0, 2 * B_WIDTH)
    vq = proj(c0 + 2 * B_WIDTH, 2 * B_WIDTH)
    kv = proj(c0 + 4 * B_WIDTH, 2 * B_WIDTH)
    u = jax.nn.gelu(pa[:, :A_WIDTH])
    v = jax.nn.gelu(pa[:, A_WIDTH:])
    grp_r = lax.broadcasted_iota(jnp.int32, (A_WIDTH, A_WIDTH), 0) // HEAD_DIM
    grp_c = lax.broadcasted_iota(jnp.int32, (A_WIDTH, A_WIDTH), 1) // HEAD_DIM
    group_ones = jnp.where(grp_r == grp_c, 1.0, 0.0).astype(BF16)
    hi, lo = _split_bf16(v * v)
    ssq = (jnp.dot(hi, group_ones, preferred_element_type=F32)
           + jnp.dot(lo, group_ones, preferred_element_type=F32))
    vn = v * lax.rsqrt(ssq * (1.0 / HEAD_DIM) + EPS) * gsgu_ref[...]
    t_idx = lax.broadcasted_iota(jnp.int32, (CHUNK, A_GROUPS * CHUNK), 0)
    s_idx = lax.broadcasted_iota(jnp.int32, (CHUNK, A_GROUPS * CHUNK), 1) % CHUNK
    w_causal = jnp.where(s_idx <= t_idx, wsp_ref[...], 0.0).astype(BF16)
    lane_grp = lax.broadcasted_iota(jnp.int32, (CHUNK, A_WIDTH), 1) // HEAD_DIM
    for c in range(tm // CHUNK):
        rows = slice(c * CHUNK, (c + 1) * CHUNK)
        vc = vn[rows]
        stacked = jnp.concatenate(
            [jnp.where(lane_grp == g, vc, 0.0) for g in range(A_GROUPS)], axis=0).astype(BF16)
        mixed = jnp.dot(w_causal, stacked, preferred_element_type=F32) + bsp_ref[...]
        ya_ref[rows, :] = (u[rows] * mixed).astype(BF16)

    cos = cos_ref[...]
    sin = sin_ref[...]
    first_half = (lax.broadcasted_iota(jnp.int32, (tm, LANES), 1) % HEAD_DIM) < HEAD_DIM // 2

    def rope(pc):
        outs = []
        for p in range(C_WIDTH // LANES):
            xp = pc[:, p * LANES:(p + 1) * LANES]
            rot = jnp.where(first_half,
                            pltpu.roll(xp, LANES - HEAD_DIM // 2, axis=1),
                            pltpu.roll(xp, HEAD_DIM // 2, axis=1))
            outs.append(xp * cos + rot * sin)
        return jnp.concatenate(outs, axis=1)

    bq_ref[...] = (qk[:, :B_WIDTH] * (scale * LOG2_E)).astype(BF16)
    bk_ref[...] = qk[:, B_WIDTH:].astype(BF16)
    bv_ref[...] = vq[:, :B_WIDTH].astype(BF16)
    cq_ref[...] = rope(vq[:, B_WIDTH:]) * scale
    ck_ref[...] = rope(kv[:, :C_WIDTH])
    cv_ref[...] = kv[:, C_WIDTH:]


def _in_projection(x2d, mod_l, g_mix, w_in, cos, sin, g_sgu, w_sp, b_sp, *, layer, seq, tm):
    tokens, d = x2d.shape
    per_batch = seq // tm
    in_width = w_in.shape[-1]
    w_cat = jnp.transpose(w_sp, (1, 0, 2)).reshape(CHUNK, A_GROUPS * CHUNK)
    bias = jnp.repeat(b_sp.T, HEAD_DIM, axis=1)
    tile = lambda w: pl.BlockSpec((tm, w), lambda i: (i, 0))
    full = lambda a: pl.BlockSpec(a.shape, lambda i: (0,) * a.ndim)
    g_mix = g_mix.reshape(1, d)
    g_sgu = g_sgu.reshape(1, A_WIDTH)
    outs = pl.pallas_call(
        functools.partial(_inproj_kernel, tm=tm),
        grid=(tokens // tm,),
        in_specs=[tile(d),
                  pl.BlockSpec((1, N_MOD, d), lambda i: (i // per_batch, 0, 0)),
                  full(g_mix), _layer_weight(w_in, layer), tile(LANES), tile(LANES),
                  full(g_sgu), full(w_cat), full(bias)],
        out_specs=[tile(A_WIDTH)] + [tile(B_WIDTH)] * 3 + [tile(C_WIDTH)] * 3,
        out_shape=[jax.ShapeDtypeStruct((tokens, w), dt) for w, dt in
                   [(A_WIDTH, BF16)] + [(B_WIDTH, BF16)] * 3 + [(C_WIDTH, F32)] * 3],
        scratch_shapes=[pltpu.VMEM((tm, d), BF16)],
        compiler_params=_params("parallel"),
        name="in_projection",
    )(x2d, mod_l, g_mix, w_in, cos, sin, g_sgu, w_cat, bias)
    assert in_width == 2 * A_WIDTH + 3 * B_WIDTH + 3 * C_WIDTH
    return outs


def _stick_kernel(q_ref, k_ref, v_ref, later_ref, o_ref, acc_scr, carry_scr, top_scr, *, blk, lead,
                  n_q):
    first_q = pl.program_id(2) * n_q
    rows = HEADS_PER_TILE * blk
    lane_head = lax.broadcasted_iota(jnp.int32, (blk, LANES), 1) // HEAD_DIM
    t_idx = lax.broadcasted_iota(jnp.int32, (rows, blk), 0) % blk
    before = lax.broadcasted_iota(jnp.int32, (rows, blk), 1) < t_idx

    def stacked_q(j):
        q = q_ref[j * blk:(j + 1) * blk, :]
        return jnp.concatenate(
            [jnp.where(lane_head == h, q, jnp.zeros_like(q)) for h in range(HEADS_PER_TILE)], axis=0)

    def visit(q_stack, first_block, n_blocks, diagonal, carry, acc, out):
        start = pl.multiple_of(first_block * blk, blk)
        k_win = k_ref[pl.ds(start, n_blocks * blk), :]
        v_win = v_ref[pl.ds(start, n_blocks * blk), :]
        z = lax.dot_general(q_stack, k_win, (((1,), (1,)), ((), ())), preferred_element_type=F32)
        yield
        log_beta = jnp.minimum(z, 0.0) - jnp.log2(1.0 + jnp.exp2(-jnp.abs(z)))
        log_stay = log_beta - z
        parts = []
        for w in range(n_blocks):
            stay_w = log_stay[:, w * blk:(w + 1) * blk]
            if diagonal and w == n_blocks - 1:
                stay_w = jnp.where(before, stay_w, 0.0)
            parts.append(jnp.concatenate(_split_bf16(stay_w), axis=1))
        sums = jnp.dot(jnp.concatenate(parts, axis=0), later_ref[...], preferred_element_type=F32)
        yield
        weights = [None] * n_blocks
        for w in reversed(range(n_blocks)):
            sums_w = sums[w * rows:(w + 1) * rows]
            weight = jnp.exp2(log_beta[:, w * blk:(w + 1) * blk] + sums_w[:, :blk] + carry)
            if diagonal and w == n_blocks - 1:
                weight = jnp.where(before, weight, 0.0)
            weights[w] = weight.astype(BF16)
            carry = carry + sums_w[:, blk:]
        acc = acc + jnp.dot(jnp.concatenate(weights, axis=1), v_win, preferred_element_type=F32)
        out.append((carry, acc))

    def run_staged(visits):
        while visits:
            visits = [g for g in visits if next(g, True) is None]

    def first_visits(first_q, n_blocks):
        zero = jnp.zeros((rows, blk), F32)
        outs = [[] for _ in range(n_q)]
        run_staged([visit(stacked_q(j), first_q + j - (n_blocks[j] - 1), n_blocks[j], True,
                          zero, zero, outs[j]) for j in range(n_q)])
        for j in range(n_q):
            carry_scr[j], acc_scr[j] = outs[j][0]
        tops = [jnp.max(outs[j][0][0], axis=0, keepdims=True)
                for j in range(n_q) if n_blocks[j] == lead]
        top = functools.reduce(jnp.maximum, tops) if tops else jnp.full((1, blk), -jnp.inf, F32)
        top_scr[...] = jnp.broadcast_to(top, top_scr.shape)

    @pl.when(first_q == 0)
    def _():
        first_visits(0, [min(j + 1, lead) for j in range(n_q)])

    @pl.when(first_q > 0)
    def _():
        first_visits(first_q, [lead] * n_q)

    def alive(carry):
        top = jnp.max(carry, axis=0, keepdims=True)
        return (top[0, 0] > EXP2_ZERO_BELOW).astype(jnp.int32)

    def finish(j):
        def cond(state):
            kb, go = state
            return jnp.logical_and(kb >= 0, go > 0)

        def body(state):
            kb, _ = state
            out = []
            run_staged([visit(stacked_q(j), kb, 1, False, carry_scr[j], acc_scr[j], out)])
            carry_scr[j], acc_scr[j] = out[0]
            return kb - 1, alive(carry_scr[j])

        lax.while_loop(cond, body, (first_q + j - lead, alive(carry_scr[j])))

    @pl.when(alive(top_scr[...]) > 0)
    def _():
        for j in range(n_q):
            finish(j)

    for j in range(n_q):
        o_ref[j * blk:(j + 1) * blk, :] = jnp.where(
            lane_head == 0, acc_scr[j, :blk], acc_scr[j, blk:]).astype(BF16)


def _stick_breaking(q, k, v, *, bsz, seq):
    tokens, width = q.shape
    blk = Q_BLOCK
    n_q = STICK_Q_BLOCKS_PER_STEP
    assert n_q >= STICK_LEAD_BLOCKS - 1
    steps = seq // (blk * n_q)
    j_idx = lax.broadcasted_iota(jnp.int32, (2 * blk, 2 * blk), 0) % blk
    s_idx = lax.broadcasted_iota(jnp.int32, (2 * blk, 2 * blk), 1)
    later = jnp.where((j_idx > s_idx) | (s_idx >= blk), 1.0, 0.0).astype(BF16)
    q_spec = pl.BlockSpec((n_q * blk, LANES), lambda b, p, i: (b * steps + i, p))
    kv_spec = pl.BlockSpec((seq, LANES), lambda b, p, i: (b, p))
    return pl.pallas_call(
        functools.partial(_stick_kernel, blk=blk, lead=STICK_LEAD_BLOCKS, n_q=n_q),
        grid=(bsz, width // LANES, steps),
        in_specs=[q_spec, kv_spec, kv_spec, pl.BlockSpec(later.shape, lambda b, p, i: (0, 0))],
        out_specs=q_spec,
        out_shape=jax.ShapeDtypeStruct((tokens, width), BF16),
        scratch_shapes=[pltpu.VMEM((n_q, HEADS_PER_TILE * blk, LANES), F32),
                        pltpu.VMEM((n_q, HEADS_PER_TILE * blk, blk), F32),
                        pltpu.VMEM((8, blk), F32)],
        compiler_params=_params("parallel", "parallel", "parallel"),
        name="stick_breaking",
    )(q, k, v, later)


def _window_kernel(q_ref, k_ref, v_ref, o_ref, lse_ref, *, seq, max_tq):
    pair = pl.program_id(1)
    rows = HEADS_PER_TILE * WINDOW
    lane_head = lax.broadcasted_iota(jnp.int32, (WINDOW, LANES), 1) // HEAD_DIM
    off = (lax.broadcasted_iota(jnp.int32, (rows, 2 * WINDOW), 1)
           - lax.broadcasted_iota(jnp.int32, (rows, 2 * WINDOW), 0) % WINDOW)
    band = (off >= 0) & (off <= WINDOW)
    from_prev = lax.broadcasted_iota(jnp.int32, (rows, 2 * WINDOW), 1) < WINDOW

    def class_rows(first, d):
        return pl.ds(first, WINDOW) if d == 1 else pl.ds(first, WINDOW, stride=d)

    def block(d, r, j, prev, out):
        cur = class_rows(r + d * j, d)
        q = q_ref[cur, :].astype(BF16)
        k_cur = k_ref[cur, :].astype(BF16)
        v_cur = v_ref[cur, :].astype(BF16)
        out.append((k_cur, v_cur))
        if prev is None:
            before = class_rows(r + d * jnp.maximum(j - WINDOW, 0), d)
            prev = (k_ref[before, :].astype(BF16), v_ref[before, :].astype(BF16))
            valid = band & (jnp.logical_not(from_prev) | (j > 0))
        else:
            valid = band
        q_stack = jnp.concatenate(
            [jnp.where(lane_head == h, q, jnp.zeros_like(q)) for h in range(HEADS_PER_TILE)], axis=0)
        z = lax.dot_general(q_stack, jnp.concatenate([prev[0], k_cur], axis=0),
                            (((1,), (1,)), ((), ())), preferred_element_type=F32)
        yield
        z = jnp.where(valid, z, -jnp.inf)
        z_max = jnp.max(z, axis=-1, keepdims=True)
        prob = jnp.exp(z - z_max)
        denom = jnp.sum(prob, axis=-1, keepdims=True)
        pv = jnp.dot(prob.astype(BF16), jnp.concatenate([prev[1], v_cur], axis=0),
                     preferred_element_type=F32)
        yield
        o = pv * (1.0 / denom)
        lse = z_max + jnp.log(denom)
        o_ref[cur, :] = jnp.where(lane_head == 0, o[:WINDOW], o[WINDOW:])
        lse_ref[cur, :] = jnp.where(lane_head == 0, lse[:WINDOW], lse[WINDOW:])

    def tiles(d):
        tq = min(max_tq, seq // d)
        runs = max_tq // tq
        per_class = seq // d // tq

        def tile(i, _):
            visits = []
            for c in range(runs):
                run = i * runs + c
                r = run // per_class
                j0 = (run % per_class) * tq
                prev = None
                for s in range(tq // WINDOW):
                    out = []
                    visits.append(block(d, r, j0 + s * WINDOW, prev, out))
                    next(visits[-1])
                    prev = out[0]
            while visits:
                visits = [g for g in visits if next(g, True) is None]
            return 0

        lax.fori_loop(0, seq // max_tq, tile, 0)

    for idx, d in enumerate(DILATIONS):
        pl.when(pair == idx)(functools.partial(tiles, d))


def _dilated_window(q, k, v, *, bsz, seq):
    tokens, width = q.shape
    spec = pl.BlockSpec((seq, LANES), lambda b, p: (b, p))
    return pl.pallas_call(
        functools.partial(_window_kernel, seq=seq, max_tq=WINDOW_TILE_ROWS),
        grid=(bsz, width // LANES),
        in_specs=[spec] * 3,
        out_specs=[spec] * 2,
        out_shape=[jax.ShapeDtypeStruct((tokens, width), F32)] * 2,
        compiler_params=_params("parallel", "parallel"),
        name="dilated_window",
    )(q, k, v)


def _layer_tail_kernel(x_ref, mod_ref, ya_ref, yb_ref, oc_ref, lse_ref, wout_ref, g_ref, wup_ref,
                       cw_ref, cb_ref, wdn_ref, gfin_ref, o_ref,
                       y_scr, x_scr, tail_scr, h_scr, up_scr, act_scr, down_scr, *, tm, tn, per_batch,
                       final):
    i = pl.program_id(0)
    mod = mod_ref[0]
    d_ff = wdn_ref.shape[0]

    y_scr[:, :A_WIDTH] = ya_ref[...]
    y_scr[:, A_WIDTH:A_WIDTH + B_WIDTH] = yb_ref[...]
    pairs = C_WIDTH // LANES
    lse = [lse_ref[:, p * LANES:(p + 1) * LANES] for p in range(pairs)]
    top = functools.reduce(jnp.maximum, lse)
    share = [jnp.exp(l - top) for l in lse]
    inv = 1.0 / functools.reduce(lambda a, b: a + b, share)
    c0 = A_WIDTH + B_WIDTH
    for p in range(pairs):
        o_p = oc_ref[:, p * LANES:(p + 1) * LANES]
        y_scr[:, c0 + p * LANES:c0 + (p + 1) * LANES] = (o_p * (share[p] * inv)).astype(BF16)
    mixed = jnp.dot(y_scr[...], wout_ref[...], preferred_element_type=F32)
    x_scr[...] = x_ref[...] + (1.0 + mod[2:3]) * mixed

    def norm_mod(xv):
        return _rms_scale(xv) * g_ref[...] * (1.0 + mod[4:5]) + mod[3:4]

    opens_sequence = i % per_batch == 0

    @pl.when(opens_sequence)
    def _():
        tail_scr[...] = jnp.zeros_like(tail_scr)

    halo = jnp.where(opens_sequence, 0.0, norm_mod(tail_scr[...]))
    h_scr[:CONV_HALO, :] = halo.astype(BF16)
    h_scr[CONV_HALO:, :] = norm_mod(x_scr[...]).astype(BF16)
    tail_scr[...] = x_scr[tm - CONV_HALO:, :]

    half = tm // 2

    slabs = range(tn // LANES)
    n_tiles = d_ff // tn
    in_flight = up_scr.shape[0] // 2

    def project(n):
        for part in range(2):
            c0 = part * d_ff + n * tn
            up = jnp.dot(h_scr[...], wup_ref[:, c0:c0 + tn], preferred_element_type=F32)
            for s in slabs:
                up_scr[2 * (n % in_flight) + part, s] = up[:, s * LANES:(s + 1) * LANES]

    def conv(slot, c0):
        w = cw_ref[:, c0:c0 + tn]
        bias = cb_ref[:, c0:c0 + tn]
        parities = []
        for parity in range(2):
            cols = []
            for s in slabs:
                lanes = slice(s * LANES, (s + 1) * LANES)
                out = bias[:, lanes]
                for tap in range(CONV_WIDTH):
                    first = CONV_HALO - (CONV_WIDTH - 1) + tap + parity
                    out = out + w[tap:tap + 1, lanes] * up_scr[slot, s, pl.ds(first, half, stride=2), :]
                cols.append(out)
            parities.append(jnp.concatenate(cols, axis=1))
        return jnp.concatenate(parities, axis=0)

    project(0)
    for n in range(n_tiles):
        if n + 1 < n_tiles:
            project(n + 1)
        slot = 2 * (n % in_flight)
        gate = conv(slot, n * tn)
        val = conv(slot + 1, d_ff + n * tn)
        act = gate * (1.0 / (1.0 + jnp.exp(-gate))) * val
        act_scr[:, n * tn:(n + 1) * tn] = act.astype(BF16)
    down = jnp.dot(act_scr[...], wdn_ref[...], preferred_element_type=F32)
    for s in range(down_scr.shape[0]):
        lanes = slice(s * LANES, (s + 1) * LANES)
        down_scr[s, pl.ds(0, half, stride=2), :] = down[:half, lanes]
        down_scr[s, pl.ds(1, half, stride=2), :] = down[half:, lanes]
    down = jnp.concatenate([down_scr[s] for s in range(down_scr.shape[0])], axis=1)
    x_new = x_scr[...] + (1.0 + mod[5:6]) * down
    if final:
        x_new = _rms_scale(x_new) * gfin_ref[...]
    o_ref[...] = x_new


def _layer_tail(x2d, mod_l, y_a, y_b, o_c, lse, w_out, g_ffn, w_up, conv_w, conv_b, w_down, g_final,
                *, layer, seq, tm, tn, final):
    tokens, d = x2d.shape
    per_batch = seq // tm
    tile = lambda w: pl.BlockSpec((tm, w), lambda i: (i, 0))
    full = lambda a: pl.BlockSpec(a.shape, lambda i: (0,) * a.ndim)
    resident = lambda a: _layer_weight(a, layer)
    g_ffn = g_ffn.reshape(1, d)
    conv_b = conv_b.reshape(1, -1)
    g_final = g_final.reshape(1, d)
    return pl.pallas_call(
        functools.partial(_layer_tail_kernel, tm=tm, tn=tn, per_batch=per_batch, final=final),
        grid=(tokens // tm,),
        in_specs=[tile(d),
                  pl.BlockSpec((1, N_MOD, d), lambda i: (i // per_batch, 0, 0)),
                  tile(A_WIDTH), tile(B_WIDTH), tile(C_WIDTH), tile(C_WIDTH), resident(w_out),
                  full(g_ffn), resident(w_up), full(conv_w), full(conv_b), resident(w_down),
                  full(g_final)],
        out_specs=tile(d),
        out_shape=jax.ShapeDtypeStruct((tokens, d), F32),
        scratch_shapes=[pltpu.VMEM((tm, w_out.shape[1]), BF16),
                        pltpu.VMEM((tm, d), F32),
                        pltpu.VMEM((CONV_HALO, d), F32),
                        pltpu.VMEM((tm + CONV_HALO, d), BF16),
                        pltpu.VMEM((4, tn // LANES, tm + CONV_HALO, LANES), F32),
                        pltpu.VMEM((tm, w_down.shape[1]), BF16),
                        pltpu.VMEM((d // LANES, tm, LANES), F32)],
        compiler_params=_params("arbitrary"),
        name="layer_tail",
    )(x2d, mod_l, y_a, y_b, o_c, lse, w_out, g_ffn, w_up, conv_w, conv_b, w_down, g_final)


def kernel(x, c, positions, w_ada, b_ada, g_mix, w_in, g_sgu, w_sp, b_sp, w_out, g_ffn, w_up,
           conv_w, conv_b, w_down, g_final):
    bsz, seq, d = x.shape
    depth = w_ada.shape[0]
    tm = min(512, seq)
    assert seq % tm == 0 and seq % (max(DILATIONS) * WINDOW) == 0
    mod = _modulation(c, w_ada, b_ada)
    cos, sin = _rope_tables(positions)
    cos = cos.reshape(bsz * seq, LANES)
    sin = sin.reshape(bsz * seq, LANES)
    x2d = x.reshape(bsz * seq, d)
    w_in, w_out, w_up, w_down = (w.astype(BF16) for w in (w_in, w_out, w_up, w_down))
    for l in range(depth):
        y_a, b_q, b_k, b_v, c_q, c_k, c_v = _in_projection(
            x2d, mod[l], g_mix[l], w_in, cos, sin, g_sgu[l], w_sp[l], b_sp[l],
            layer=l, seq=seq, tm=tm)
        y_b = _stick_breaking(b_q, b_k, b_v, bsz=bsz, seq=seq)
        o_c, lse = _dilated_window(c_q, c_k, c_v, bsz=bsz, seq=seq)
        x2d = _layer_tail(x2d, mod[l], y_a, y_b, o_c, lse, w_out, g_ffn[l], w_up, conv_w[l],
                          conv_b[l], w_down, g_final, layer=l, seq=seq, tm=tm, tn=256,
                          final=(l == depth - 1))
    return x2d.reshape(bsz, seq, d)
```

```python
import functools

import jax
import jax.numpy as jnp
from jax import lax
from jax.experimental import pallas as pl
from jax.experimental.pallas import tpu as pltpu

F32 = jnp.float32
BF16 = jnp.bfloat16

LANES = 128
BF16_ROWS = 16
MXU_COLUMNS = 256
HEAD_DIM = 64
HEADS_PER_TILE = LANES // HEAD_DIM
A_GROUPS = 4
A_WIDTH = A_GROUPS * HEAD_DIM
CHUNK = 128
B_WIDTH = 6 * HEAD_DIM
C_WIDTH = 6 * HEAD_DIM
DILATIONS = (1, 4, 16)
WINDOW = 128
WINDOW_TILE_ROWS = 512
Q_BLOCK = 128
STICK_LEAD_BLOCKS = 3
STICK_Q_BLOCKS_PER_STEP = 16
CONV_WIDTH = 3
CONV_HALO = 8
ROPE_THETA = 10000.0
EPS = 1e-6
N_MOD = 6
LOG2_E = 1.4426950408889634
EXP2_ZERO_BELOW = -151.0
VMEM_LIMIT_BYTES = 56 * 1024 * 1024


def _params(*semantics):
    return pltpu.CompilerParams(dimension_semantics=semantics,
                                vmem_limit_bytes=VMEM_LIMIT_BYTES)


def _resident(w):
    return pl.BlockSpec(w.shape, lambda i: (0, 0), pipeline_mode=pl.Buffered(1))


def _cast_specs(stacked, layer, steps):
    _, rows, cols = stacked.shape
    repeat = next(g for g in (1, 2, 4, 8, 16) if rows * g % (steps * BF16_ROWS) == 0)
    slab = rows * repeat // steps
    return (pl.BlockSpec((None, slab, cols), lambda i: (layer, i // repeat, 0)),
            pl.BlockSpec((slab, cols), lambda i: (i // repeat, 0)),
            jax.ShapeDtypeStruct((rows, cols), BF16))


def _split_bf16(a):
    hi = a.astype(BF16)
    lo = (a - hi.astype(F32)).astype(BF16)
    return hi, lo


def _rms_scale(x):
    return x * lax.rsqrt(jnp.mean(x * x, axis=-1, keepdims=True) + EPS)


def _mod_kernel(c_ref, w_ref, b_ref, o_ref):
    c = c_ref[...]
    c_act = c * (1.0 / (1.0 + jnp.exp(-c)))
    o_ref[0] = jnp.dot(c_act, w_ref[0], preferred_element_type=F32) + b_ref[0]


def _modulation(c, w_ada, b_ada):
    depth, d, nd = w_ada.shape
    bsz = c.shape[0]
    rows = -(-bsz // 8) * 8
    c_pad = jnp.zeros((rows, d), F32).at[:bsz].set(c)
    out = pl.pallas_call(
        _mod_kernel,
        grid=(depth, nd // d),
        in_specs=[pl.BlockSpec((rows, d), lambda l, j: (0, 0)),
                  pl.BlockSpec((1, d, d), lambda l, j: (l, 0, j)),
                  pl.BlockSpec((1, 1, d), lambda l, j: (l, 0, j))],
        out_specs=pl.BlockSpec((1, rows, d), lambda l, j: (l, 0, j)),
        out_shape=jax.ShapeDtypeStruct((depth, rows, nd), F32),
        compiler_params=_params("parallel", "parallel"),
        name="adaln_modulation",
    )(c_pad, w_ada, b_ada.reshape(depth, 1, nd))
    return out[:, :bsz].reshape(depth, bsz, N_MOD, d)


def _rope_kernel(pos_ref, freq_ref, sign_ref, cos_ref, sin_ref):
    ang = pos_ref[0] * freq_ref[...]
    cos_ref[0] = jnp.cos(ang)
    sin_ref[0] = jnp.sin(ang) * sign_ref[...]


def _rope_tables(positions):
    bsz, seq = positions.shape
    half = HEAD_DIM // 2
    inv_freq = ROPE_THETA ** (-jnp.arange(0, HEAD_DIM, 2, dtype=F32) / HEAD_DIM)
    freq = jnp.tile(inv_freq, LANES // half).reshape(1, LANES)
    sign = jnp.tile(jnp.concatenate([-jnp.ones((half,), F32), jnp.ones((half,), F32)]),
                    HEADS_PER_TILE).reshape(1, LANES)
    pos = positions.astype(F32).reshape(bsz, seq, 1)
    row = pl.BlockSpec((1, LANES), lambda b: (0, 0))
    tab = pl.BlockSpec((1, seq, LANES), lambda b: (b, 0, 0))
    return pl.pallas_call(
        _rope_kernel,
        grid=(bsz,),
        in_specs=[pl.BlockSpec((1, seq, 1), lambda b: (b, 0, 0)), row, row],
        out_specs=[tab, tab],
        out_shape=[jax.ShapeDtypeStruct((bsz, seq, LANES), F32)] * 2,
        compiler_params=_params("parallel"),
        name="rope_tables",
    )(pos, freq, sign)


def _inproj_kernel(x_ref, mod_ref, g_ref, w_ref, cos_ref, sin_ref, gsgu_ref, wsp_ref, bsp_ref,
                   *rest, tm, n_cast):
    cast_from, rest = rest[:n_cast], rest[n_cast:]
    ya_ref, bq_ref, bk_ref, bv_ref, cq_ref, ck_ref, cv_ref = rest[:7]
    cast_to, (h_scr,) = rest[7:7 + n_cast], rest[7 + n_cast:]
    for src_ref, dst_ref in zip(cast_from, cast_to):
        dst_ref[...] = src_ref[...].astype(BF16)
    mod = mod_ref[0]
    h = _rms_scale(x_ref[...]) * g_ref[...] * (1.0 + mod[1:2]) + mod[0:1]
    h_scr[...] = h.astype(BF16)
    scale = HEAD_DIM ** -0.5

    def proj(c0, width):
        return jnp.dot(h_scr[...], w_ref[:, c0:c0 + width], preferred_element_type=F32)

    assert B_WIDTH == C_WIDTH and (2 * B_WIDTH) % MXU_COLUMNS == 0
    pa = proj(0, 2 * A_WIDTH)
    c0 = 2 * A_WIDTH
    qk = proj(c0, 2 * B_WIDTH)
    vq = proj(c0 + 2 * B_WIDTH, 2 * B_WIDTH)
    kv = proj(c0 + 4 * B_WIDTH, 2 * B_WIDTH)
    u = jax.nn.gelu(pa[:, :A_WIDTH])
    v = jax.nn.gelu(pa[:, A_WIDTH:])
    grp_r = lax.broadcasted_iota(jnp.int32, (A_WIDTH, A_WIDTH), 0) // HEAD_DIM
    grp_c = lax.broadcasted_iota(jnp.int32, (A_WIDTH, A_WIDTH), 1) // HEAD_DIM
    group_ones = jnp.where(grp_r == grp_c, 1.0, 0.0).astype(BF16)
    hi, lo = _split_bf16(v * v)
    ssq = (jnp.dot(hi, group_ones, preferred_element_type=F32)
           + jnp.dot(lo, group_ones, preferred_element_type=F32))
    vn = v * lax.rsqrt(ssq * (1.0 / HEAD_DIM) + EPS) * gsgu_ref[...]
    t_idx = lax.broadcasted_iota(jnp.int32, (CHUNK, A_GROUPS * CHUNK), 0)
    s_idx = lax.broadcasted_iota(jnp.int32, (CHUNK, A_GROUPS * CHUNK), 1) % CHUNK
    w_causal = jnp.where(s_idx <= t_idx, wsp_ref[...], 0.0).astype(BF16)
    lane_grp = lax.broadcasted_iota(jnp.int32, (CHUNK, A_WIDTH), 1) // HEAD_DIM
    for c in range(tm // CHUNK):
        rows = slice(c * CHUNK, (c + 1) * CHUNK)
        vc = vn[rows]
        stacked = jnp.concatenate(
            [jnp.where(lane_grp == g, vc, 0.0) for g in range(A_GROUPS)], axis=0).astype(BF16)
        mixed = jnp.dot(w_causal, stacked, preferred_element_type=F32) + bsp_ref[...]
        ya_ref[rows, :] = (u[rows] * mixed).astype(BF16)

    cos = cos_ref[...]
    sin = sin_ref[...]
    first_half = (lax.broadcasted_iota(jnp.int32, (tm, LANES), 1) % HEAD_DIM) < HEAD_DIM // 2

    def rope(pc):
        outs = []
        for p in range(C_WIDTH // LANES):
            xp = pc[:, p * LANES:(p + 1) * LANES]
            rot = jnp.where(first_half,
                            pltpu.roll(xp, LANES - HEAD_DIM // 2, axis=1),
                            pltpu.roll(xp, HEAD_DIM // 2, axis=1))
            outs.append(xp * cos + rot * sin)
        return jnp.concatenate(outs, axis=1)

    bq_ref[...] = (qk[:, :B_WIDTH] * (scale * LOG2_E)).astype(BF16)
    bk_ref[...] = qk[:, B_WIDTH:].astype(BF16)
    bv_ref[...] = vq[:, :B_WIDTH].astype(BF16)
    cq_ref[...] = rope(vq[:, B_WIDTH:]) * scale
    ck_ref[...] = rope(kv[:, :C_WIDTH])
    cv_ref[...] = kv[:, C_WIDTH:]


def _in_projection(x2d, mod_l, g_mix, w_in, cos, sin, g_sgu, w_sp, b_sp, to_cast, *, layer, seq, tm):
    tokens, d = x2d.shape
    per_batch = seq // tm
    in_width = w_in.shape[-1]
    w_cat = jnp.transpose(w_sp, (1, 0, 2)).reshape(CHUNK, A_GROUPS * CHUNK)
    bias = jnp.repeat(b_sp.T, HEAD_DIM, axis=1)
    tile = lambda w: pl.BlockSpec((tm, w), lambda i: (i, 0))
    full = lambda a: pl.BlockSpec(a.shape, lambda i: (0,) * a.ndim)
    g_mix = g_mix.reshape(1, d)
    g_sgu = g_sgu.reshape(1, A_WIDTH)
    steps = tokens // tm
    casts = [_cast_specs(w, layer, steps) for w in to_cast]
    outs = pl.pallas_call(
        functools.partial(_inproj_kernel, tm=tm, n_cast=len(casts)),
        grid=(steps,),
        in_specs=[tile(d),
                  pl.BlockSpec((1, N_MOD, d), lambda i: (i // per_batch, 0, 0)),
                  full(g_mix), _resident(w_in), tile(LANES), tile(LANES),
                  full(g_sgu), full(w_cat), full(bias)] + [c[0] for c in casts],
        out_specs=[tile(A_WIDTH)] + [tile(B_WIDTH)] * 3 + [tile(C_WIDTH)] * 3 + [c[1] for c in casts],
        out_shape=[jax.ShapeDtypeStruct((tokens, w), dt) for w, dt in
                   [(A_WIDTH, BF16)] + [(B_WIDTH, BF16)] * 3 + [(C_WIDTH, F32)] * 3]
        + [c[2] for c in casts],
        scratch_shapes=[pltpu.VMEM((tm, d), BF16)],
        compiler_params=_params("arbitrary"),
        name="in_projection",
    )(x2d, mod_l, g_mix, w_in, cos, sin, g_sgu, w_cat, bias, *to_cast)
    assert in_width == 2 * A_WIDTH + 3 * B_WIDTH + 3 * C_WIDTH
    return outs[:7], outs[7:]


def _stick_kernel(q_ref, k_ref, v_ref, later_ref, o_ref, acc_scr, carry_scr, top_scr, *, blk, lead,
                  n_q):
    first_q = pl.program_id(2) * n_q
    rows = HEADS_PER_TILE * blk
    lane_head = lax.broadcasted_iota(jnp.int32, (blk, LANES), 1) // HEAD_DIM
    t_idx = lax.broadcasted_iota(jnp.int32, (rows, blk), 0) % blk
    before = lax.broadcasted_iota(jnp.int32, (rows, blk), 1) < t_idx

    def stacked_q(j):
        q = q_ref[j * blk:(j + 1) * blk, :]
        return jnp.concatenate(
            [jnp.where(lane_head == h, q, jnp.zeros_like(q)) for h in range(HEADS_PER_TILE)], axis=0)

    def visit(q_stack, first_block, n_blocks, diagonal, carry, acc, out):
        start = pl.multiple_of(first_block * blk, blk)
        k_win = k_ref[pl.ds(start, n_blocks * blk), :]
        v_win = v_ref[pl.ds(start, n_blocks * blk), :]
        z = lax.dot_general(q_stack, k_win, (((1,), (1,)), ((), ())), preferred_element_type=F32)
        yield
        log_beta = jnp.minimum(z, 0.0) - jnp.log2(1.0 + jnp.exp2(-jnp.abs(z)))
        log_stay = log_beta - z
        parts = []
        for w in range(n_blocks):
            stay_w = log_stay[:, w * blk:(w + 1) * blk]
            if diagonal and w == n_blocks - 1:
                stay_w = jnp.where(before, stay_w, 0.0)
            parts.append(jnp.concatenate(_split_bf16(stay_w), axis=1))
        sums = jnp.dot(jnp.concatenate(parts, axis=0), later_ref[...], preferred_element_type=F32)
        yield
        weights = [None] * n_blocks
        for w in reversed(range(n_blocks)):
            sums_w = sums[w * rows:(w + 1) * rows]
            weight = jnp.exp2(log_beta[:, w * blk:(w + 1) * blk] + sums_w[:, :blk] + carry)
            if diagonal and w == n_blocks - 1:
                weight = jnp.where(before, weight, 0.0)
            weights[w] = weight.astype(BF16)
            carry = carry + sums_w[:, blk:]
        acc = acc + jnp.dot(jnp.concatenate(weights, axis=1), v_win, preferred_element_type=F32)
        out.append((carry, acc))

    def run_staged(visits):
        while visits:
            visits = [g for g in visits if next(g, True) is None]

    def first_visits(first_q, n_blocks):
        zero = jnp.zeros((rows, blk), F32)
        outs = [[] for _ in range(n_q)]
        run_staged([visit(stacked_q(j), first_q + j - (n_blocks[j] - 1), n_blocks[j], True,
                          zero, zero, outs[j]) for j in range(n_q)])
        for j in range(n_q):
            carry_scr[j], acc_scr[j] = outs[j][0]
        tops = [jnp.max(outs[j][0][0], axis=0, keepdims=True)
                for j in range(n_q) if n_blocks[j] == lead]
        top = functools.reduce(jnp.maximum, tops) if tops else jnp.full((1, blk), -jnp.inf, F32)
        top_scr[...] = jnp.broadcast_to(top, top_scr.shape)

    @pl.when(first_q == 0)
    def _():
        first_visits(0, [min(j + 1, lead) for j in range(n_q)])

    @pl.when(first_q > 0)
    def _():
        first_visits(first_q, [lead] * n_q)

    def alive(carry):
        top = jnp.max(carry, axis=0, keepdims=True)
        return (top[0, 0] > EXP2_ZERO_BELOW).astype(jnp.int32)

    def finish(j):
        def cond(state):
            kb, go = state
            return jnp.logical_and(kb >= 0, go > 0)

        def body(state):
            kb, _ = state
            out = []
            run_staged([visit(stacked_q(j), kb, 1, False, carry_scr[j], acc_scr[j], out)])
            carry_scr[j], acc_scr[j] = out[0]
            return kb - 1, alive(carry_scr[j])

        lax.while_loop(cond, body, (first_q + j - lead, alive(carry_scr[j])))

    @pl.when(alive(top_scr[...]) > 0)
    def _():
        for j in range(n_q):
            finish(j)

    for j in range(n_q):
        o_ref[j * blk:(j + 1) * blk, :] = jnp.where(
            lane_head == 0, acc_scr[j, :blk], acc_scr[j, blk:]).astype(BF16)


def _stick_breaking(q, k, v, *, bsz, seq):
    tokens, width = q.shape
    blk = Q_BLOCK
    n_q = STICK_Q_BLOCKS_PER_STEP
    assert n_q >= STICK_LEAD_BLOCKS - 1
    steps = seq // (blk * n_q)
    j_idx = lax.broadcasted_iota(jnp.int32, (2 * blk, 2 * blk), 0) % blk
    s_idx = lax.broadcasted_iota(jnp.int32, (2 * blk, 2 * blk), 1)
    later = jnp.where((j_idx > s_idx) | (s_idx >= blk), 1.0, 0.0).astype(BF16)
    q_spec = pl.BlockSpec((n_q * blk, LANES), lambda b, p, i: (b * steps + i, p))
    kv_spec = pl.BlockSpec((seq, LANES), lambda b, p, i: (b, p))
    return pl.pallas_call(
        functools.partial(_stick_kernel, blk=blk, lead=STICK_LEAD_BLOCKS, n_q=n_q),
        grid=(bsz, width // LANES, steps),
        in_specs=[q_spec, kv_spec, kv_spec, pl.BlockSpec(later.shape, lambda b, p, i: (0, 0))],
        out_specs=q_spec,
        out_shape=jax.ShapeDtypeStruct((tokens, width), BF16),
        scratch_shapes=[pltpu.VMEM((n_q, HEADS_PER_TILE * blk, LANES), F32),
                        pltpu.VMEM((n_q, HEADS_PER_TILE * blk, blk), F32),
                        pltpu.VMEM((8, blk), F32)],
        compiler_params=_params("parallel", "parallel", "parallel"),
        name="stick_breaking",
    )(q, k, v, later)


def _window_kernel(q_ref, k_ref, v_ref, o_ref, lse_ref, *, seq, max_tq):
    pair = pl.program_id(1)
    rows = HEADS_PER_TILE * WINDOW
    lane_head = lax.broadcasted_iota(jnp.int32, (WINDOW, LANES), 1) // HEAD_DIM
    off = (lax.broadcasted_iota(jnp.int32, (rows, 2 * WINDOW), 1)
           - lax.broadcasted_iota(jnp.int32, (rows, 2 * WINDOW), 0) % WINDOW)
    band = (off >= 0) & (off <= WINDOW)
    from_prev = lax.broadcasted_iota(jnp.int32, (rows, 2 * WINDOW), 1) < WINDOW

    def class_rows(first, d):
        return pl.ds(first, WINDOW) if d == 1 else pl.ds(first, WINDOW, stride=d)

    def block(d, r, j, prev, out):
        cur = class_rows(r + d * j, d)
        q = q_ref[cur, :].astype(BF16)
        k_cur = k_ref[cur, :].astype(BF16)
        v_cur = v_ref[cur, :].astype(BF16)
        out.append((k_cur, v_cur))
        if prev is None:
            before = class_rows(r + d * jnp.maximum(j - WINDOW, 0), d)
            prev = (k_ref[before, :].astype(BF16), v_ref[before, :].astype(BF16))
            valid = band & (jnp.logical_not(from_prev) | (j > 0))
        else:
            valid = band
        q_stack = jnp.concatenate(
            [jnp.where(lane_head == h, q, jnp.zeros_like(q)) for h in range(HEADS_PER_TILE)], axis=0)
        z = lax.dot_general(q_stack, jnp.concatenate([prev[0], k_cur], axis=0),
                            (((1,), (1,)), ((), ())), preferred_element_type=F32)
        yield
        z = jnp.where(valid, z, -jnp.inf)
        z_max = jnp.max(z, axis=-1, keepdims=True)
        prob = jnp.exp(z - z_max)
        denom = jnp.sum(prob, axis=-1, keepdims=True)
        pv = jnp.dot(prob.astype(BF16), jnp.concatenate([prev[1], v_cur], axis=0),
                     preferred_element_type=F32)
        yield
        o = pv * (1.0 / denom)
        lse = z_max + jnp.log(denom)
        o_ref[cur, :] = jnp.where(lane_head == 0, o[:WINDOW], o[WINDOW:])
        lse_ref[cur, :] = jnp.where(lane_head == 0, lse[:WINDOW], lse[WINDOW:])

    def tiles(d):
        tq = min(max_tq, seq // d)
        runs = max_tq // tq
        per_class = seq // d // tq

        def tile(i, _):
            visits = []
            for c in range(runs):
                run = i * runs + c
                r = run // per_class
                j0 = (run % per_class) * tq
                prev = None
                for s in range(tq // WINDOW):
                    out = []
                    visits.append(block(d, r, j0 + s * WINDOW, prev, out))
                    next(visits[-1])
                    prev = out[0]
            while visits:
                visits = [g for g in visits if next(g, True) is None]
            return 0

        lax.fori_loop(0, seq // max_tq, tile, 0)

    for idx, d in enumerate(DILATIONS):
        pl.when(pair == idx)(functools.partial(tiles, d))


def _dilated_window(q, k, v, *, bsz, seq):
    tokens, width = q.shape
    spec = pl.BlockSpec((seq, LANES), lambda b, p: (b, p))
    return pl.pallas_call(
        functools.partial(_window_kernel, seq=seq, max_tq=WINDOW_TILE_ROWS),
        grid=(bsz, width // LANES),
        in_specs=[spec] * 3,
        out_specs=[spec] * 2,
        out_shape=[jax.ShapeDtypeStruct((tokens, width), F32)] * 2,
        compiler_params=_params("parallel", "parallel"),
        name="dilated_window",
    )(q, k, v)


def _layer_tail_kernel(x_ref, mod_ref, ya_ref, yb_ref, oc_ref, lse_ref, wout_ref, g_ref, wup_ref,
                       cw_ref, cb_ref, wdn_ref, gfin_ref, *rest, tm, tn, per_batch, final):
    if final:
        o_ref, y_scr, x_scr, tail_scr, h_scr, up_scr, act_scr, down_scr = rest
    else:
        win_ref, o_ref, win_bf_ref, y_scr, x_scr, tail_scr, h_scr, up_scr, act_scr, down_scr = rest
        win_bf_ref[...] = win_ref[...].astype(BF16)
    i = pl.program_id(0)
    mod = mod_ref[0]
    d_ff = wdn_ref.shape[0]

    y_scr[:, :A_WIDTH] = ya_ref[...]
    y_scr[:, A_WIDTH:A_WIDTH + B_WIDTH] = yb_ref[...]
    pairs = C_WIDTH // LANES
    lse = [lse_ref[:, p * LANES:(p + 1) * LANES] for p in range(pairs)]
    top = functools.reduce(jnp.maximum, lse)
    share = [jnp.exp(l - top) for l in lse]
    inv = 1.0 / functools.reduce(lambda a, b: a + b, share)
    c0 = A_WIDTH + B_WIDTH
    for p in range(pairs):
        o_p = oc_ref[:, p * LANES:(p + 1) * LANES]
        y_scr[:, c0 + p * LANES:c0 + (p + 1) * LANES] = (o_p * (share[p] * inv)).astype(BF16)
    mixed = jnp.dot(y_scr[...], wout_ref[...], preferred_element_type=F32)
    x_scr[...] = x_ref[...] + (1.0 + mod[2:3]) * mixed

    def norm_mod(xv):
        return _rms_scale(xv) * g_ref[...] * (1.0 + mod[4:5]) + mod[3:4]

    opens_sequence = i % per_batch == 0

    @pl.when(opens_sequence)
    def _():
        tail_scr[...] = jnp.zeros_like(tail_scr)

    halo = jnp.where(opens_sequence, 0.0, norm_mod(tail_scr[...]))
    h_scr[:CONV_HALO, :] = halo.astype(BF16)
    h_scr[CONV_HALO:, :] = norm_mod(x_scr[...]).astype(BF16)
    tail_scr[...] = x_scr[tm - CONV_HALO:, :]

    half = tm // 2

    slabs = range(tn // LANES)
    n_tiles = d_ff // tn
    in_flight = up_scr.shape[0] // 2

    def project(n):
        for part in range(2):
            c0 = part * d_ff + n * tn
            up = jnp.dot(h_scr[...], wup_ref[:, c0:c0 + tn], preferred_element_type=F32)
            for s in slabs:
                up_scr[2 * (n % in_flight) + part, s] = up[:, s * LANES:(s + 1) * LANES]

    def conv(slot, c0):
        w = cw_ref[:, c0:c0 + tn]
        bias = cb_ref[:, c0:c0 + tn]
        parities = []
        for parity in range(2):
            cols = []
            for s in slabs:
                lanes = slice(s * LANES, (s + 1) * LANES)
                out = bias[:, lanes]
                for tap in range(CONV_WIDTH):
                    first = CONV_HALO - (CONV_WIDTH - 1) + tap + parity
                    out = out + w[tap:tap + 1, lanes] * up_scr[slot, s, pl.ds(first, half, stride=2), :]
                cols.append(out)
            parities.append(jnp.concatenate(cols, axis=1))
        return jnp.concatenate(parities, axis=0)

    project(0)
    for n in range(n_tiles):
        if n + 1 < n_tiles:
            project(n + 1)
        slot = 2 * (n % in_flight)
        gate = conv(slot, n * tn)
        val = conv(slot + 1, d_ff + n * tn)
        act = gate * (1.0 / (1.0 + jnp.exp(-gate))) * val
        act_scr[:, n * tn:(n + 1) * tn] = act.astype(BF16)
    down = jnp.dot(act_scr[...], wdn_ref[...], preferred_element_type=F32)
    for s in range(down_scr.shape[0]):
        lanes = slice(s * LANES, (s + 1) * LANES)
        down_scr[s, pl.ds(0, half, stride=2), :] = down[:half, lanes]
        down_scr[s, pl.ds(1, half, stride=2), :] = down[half:, lanes]
    down = jnp.concatenate([down_scr[s] for s in range(down_scr.shape[0])], axis=1)
    x_new = x_scr[...] + (1.0 + mod[5:6]) * down
    if final:
        x_new = _rms_scale(x_new) * gfin_ref[...]
    o_ref[...] = x_new


def _layer_tail(x2d, mod_l, y_a, y_b, o_c, lse, w_out, g_ffn, w_up, conv_w, conv_b, w_down, g_final,
                w_in_all, *, layer, seq, tm, tn, final):
    tokens, d = x2d.shape
    per_batch = seq // tm
    tile = lambda w: pl.BlockSpec((tm, w), lambda i: (i, 0))
    full = lambda a: pl.BlockSpec(a.shape, lambda i: (0,) * a.ndim)
    g_ffn = g_ffn.reshape(1, d)
    conv_b = conv_b.reshape(1, -1)
    g_final = g_final.reshape(1, d)
    steps = tokens // tm
    cast = [] if final else [_cast_specs(w_in_all, layer + 1, steps)]
    outs = pl.pallas_call(
        functools.partial(_layer_tail_kernel, tm=tm, tn=tn, per_batch=per_batch, final=final),
        grid=(steps,),
        in_specs=[tile(d),
                  pl.BlockSpec((1, N_MOD, d), lambda i: (i // per_batch, 0, 0)),
                  tile(A_WIDTH), tile(B_WIDTH), tile(C_WIDTH), tile(C_WIDTH), _resident(w_out),
                  full(g_ffn), _resident(w_up), full(conv_w), full(conv_b), _resident(w_down),
                  full(g_final)] + [c[0] for c in cast],
        out_specs=[tile(d)] + [c[1] for c in cast],
        out_shape=[jax.ShapeDtypeStruct((tokens, d), F32)] + [c[2] for c in cast],
        scratch_shapes=[pltpu.VMEM((tm, w_out.shape[0]), BF16),
                        pltpu.VMEM((tm, d), F32),
                        pltpu.VMEM((CONV_HALO, d), F32),
                        pltpu.VMEM((tm + CONV_HALO, d), BF16),
                        pltpu.VMEM((4, tn // LANES, tm + CONV_HALO, LANES), F32),
                        pltpu.VMEM((tm, w_down.shape[0]), BF16),
                        pltpu.VMEM((d // LANES, tm, LANES), F32)],
        compiler_params=_params("arbitrary"),
        name="layer_tail",
    )(x2d, mod_l, y_a, y_b, o_c, lse, w_out, g_ffn, w_up, conv_w, conv_b, w_down, g_final,
      *([] if final else [w_in_all]))
    return outs[0], (None if final else outs[1])


def kernel(x, c, positions, w_ada, b_ada, g_mix, w_in, g_sgu, w_sp, b_sp, w_out, g_ffn, w_up,
           conv_w, conv_b, w_down, g_final):
    bsz, seq, d = x.shape
    depth = w_ada.shape[0]
    tm = min(512, seq)
    assert seq % tm == 0 and seq % (max(DILATIONS) * WINDOW) == 0
    mod = _modulation(c, w_ada, b_ada)
    cos, sin = _rope_tables(positions)
    cos = cos.reshape(bsz * seq, LANES)
    sin = sin.reshape(bsz * seq, LANES)
    x2d = x.reshape(bsz * seq, d)
    w_in_l = w_in[0].astype(BF16)
    for l in range(depth):
        (y_a, b_q, b_k, b_v, c_q, c_k, c_v), (w_out_l, w_up_l, w_down_l) = _in_projection(
            x2d, mod[l], g_mix[l], w_in_l, cos, sin, g_sgu[l], w_sp[l], b_sp[l],
            (w_out, w_up, w_down), layer=l, seq=seq, tm=tm)
        y_b = _stick_breaking(b_q, b_k, b_v, bsz=bsz, seq=seq)
        o_c, lse = _dilated_window(c_q, c_k, c_v, bsz=bsz, seq=seq)
        x2d, w_in_l = _layer_tail(x2d, mod[l], y_a, y_b, o_c, lse, w_out_l, g_ffn[l], w_up_l,
                                  conv_w[l], conv_b[l], w_down_l, g_final, w_in, layer=l, seq=seq,
                                  tm=tm, tn=256, final=(l == depth - 1))
    return x2d.reshape(bsz, seq, d)
```

```python
import functools

import jax
import jax.numpy as jnp
from jax import lax
from jax.experimental import pallas as pl
from jax.experimental.pallas import tpu as pltpu

F32 = jnp.float32
BF16 = jnp.bfloat16

LANES = 128
BF16_ROWS = 16
MXU_COLUMNS = 256
HEAD_DIM = 64
HEADS_PER_TILE = LANES // HEAD_DIM
A_GROUPS = 4
A_WIDTH = A_GROUPS * HEAD_DIM
CHUNK = 128
B_WIDTH = 6 * HEAD_DIM
C_WIDTH = 6 * HEAD_DIM
DILATIONS = (1, 4, 16)
WINDOW = 128
WINDOW_TILE_ROWS = 512
Q_BLOCK = 128
STICK_LEAD_BLOCKS = 3
STICK_Q_BLOCKS_PER_STEP = 16
CONV_WIDTH = 3
CONV_HALO = 8
ROPE_THETA = 10000.0
EPS = 1e-6
N_MOD = 6
LOG2_E = 1.4426950408889634
EXP2_ZERO_BELOW = -151.0
VMEM_LIMIT_BYTES = 56 * 1024 * 1024


def _params(*semantics):
    return pltpu.CompilerParams(dimension_semantics=semantics,
                                vmem_limit_bytes=VMEM_LIMIT_BYTES)


def _resident(w):
    return pl.BlockSpec(w.shape, lambda i: (0, 0), pipeline_mode=pl.Buffered(1))


def _cast_specs(stacked, layer, steps):
    _, rows, cols = stacked.shape
    repeat = next(g for g in (1, 2, 4, 8, 16) if rows * g % (steps * BF16_ROWS) == 0)
    slab = rows * repeat // steps
    return (pl.BlockSpec((None, slab, cols), lambda i: (layer, i // repeat, 0)),
            pl.BlockSpec((slab, cols), lambda i: (i // repeat, 0)),
            jax.ShapeDtypeStruct((rows, cols), BF16))


def _split_bf16(a):
    hi = a.astype(BF16)
    lo = (a - hi.astype(F32)).astype(BF16)
    return hi, lo


def _rms_scale(x):
    return x * lax.rsqrt(jnp.mean(x * x, axis=-1, keepdims=True) + EPS)


def _mod_kernel(c_ref, w_ref, b_ref, o_ref):
    c = c_ref[...]
    c_act = c * (1.0 / (1.0 + jnp.exp(-c)))
    o_ref[0] = jnp.dot(c_act, w_ref[0], preferred_element_type=F32) + b_ref[0]


def _modulation(c, w_ada, b_ada):
    depth, d, nd = w_ada.shape
    bsz = c.shape[0]
    rows = -(-bsz // 8) * 8
    c_pad = jnp.zeros((rows, d), F32).at[:bsz].set(c)
    out = pl.pallas_call(
        _mod_kernel,
        grid=(depth, nd // d),
        in_specs=[pl.BlockSpec((rows, d), lambda l, j: (0, 0)),
                  pl.BlockSpec((1, d, d), lambda l, j: (l, 0, j)),
                  pl.BlockSpec((1, 1, d), lambda l, j: (l, 0, j))],
        out_specs=pl.BlockSpec((1, rows, d), lambda l, j: (l, 0, j)),
        out_shape=jax.ShapeDtypeStruct((depth, rows, nd), F32),
        compiler_params=_params("parallel", "parallel"),
        name="adaln_modulation",
    )(c_pad, w_ada, b_ada.reshape(depth, 1, nd))
    return out[:, :bsz].reshape(depth, bsz, N_MOD, d)


def _rope_kernel(pos_ref, freq_ref, sign_ref, cos_ref, sin_ref):
    n_freq = HEAD_DIM // 2
    per_row = LANES // n_freq
    pos = pos_ref[0]
    rows = pos.shape[0]
    lane_pos = lax.broadcasted_iota(jnp.int32, (rows, LANES), 1) // n_freq
    pos_lanes = jnp.zeros((rows, LANES), F32)
    for g in range(per_row):
        pos_lanes = jnp.where(lane_pos == g, pos[:, g:g + 1], pos_lanes)
    ang = pos_lanes * freq_ref[...]
    for table, out_ref, scale in ((jnp.cos(ang), cos_ref, None), (jnp.sin(ang), sin_ref, sign_ref[...])):
        for g in range(per_row):
            picked = jnp.where(lane_pos == g, table, 0.0)
            spread = picked
            for k in range(1, per_row):
                spread = spread + pltpu.roll(picked, k * n_freq, axis=1)
            if scale is not None:
                spread = spread * scale
            out_ref[0, pl.ds(g, rows, stride=per_row), :] = spread


def _rope_tables(positions):
    bsz, seq = positions.shape
    half = HEAD_DIM // 2
    per_row = LANES // half
    inv_freq = ROPE_THETA ** (-jnp.arange(0, HEAD_DIM, 2, dtype=F32) / HEAD_DIM)
    freq = jnp.tile(inv_freq, per_row).reshape(1, LANES)
    sign = jnp.tile(jnp.concatenate([-jnp.ones((half,), F32), jnp.ones((half,), F32)]),
                    HEADS_PER_TILE).reshape(1, LANES)
    pos = positions.astype(F32).reshape(bsz, seq // per_row, per_row)
    row = pl.BlockSpec((1, LANES), lambda b: (0, 0))
    tab = pl.BlockSpec((1, seq, LANES), lambda b: (b, 0, 0))
    return pl.pallas_call(
        _rope_kernel,
        grid=(bsz,),
        in_specs=[pl.BlockSpec((1, seq // per_row, per_row), lambda b: (b, 0, 0)), row, row],
        out_specs=[tab, tab],
        out_shape=[jax.ShapeDtypeStruct((bsz, seq, LANES), F32)] * 2,
        compiler_params=_params("parallel"),
        name="rope_tables",
    )(pos, freq, sign)


def _inproj_kernel(x_ref, mod_ref, g_ref, w_ref, cos_ref, sin_ref, gsgu_ref, wsp_ref, bsp_ref,
                   *rest, tm, n_cast):
    cast_from, rest = rest[:n_cast], rest[n_cast:]
    ya_ref, bq_ref, bk_ref, bv_ref, cq_ref, ck_ref, cv_ref = rest[:7]
    cast_to, (h_scr,) = rest[7:7 + n_cast], rest[7 + n_cast:]
    for src_ref, dst_ref in zip(cast_from, cast_to):
        dst_ref[...] = src_ref[...].astype(BF16)
    mod = mod_ref[0]
    h = _rms_scale(x_ref[...]) * g_ref[...] * (1.0 + mod[1:2]) + mod[0:1]
    h_scr[...] = h.astype(BF16)
    scale = HEAD_DIM ** -0.5

    def proj(c0, width):
        return jnp.dot(h_scr[...], w_ref[:, c0:c0 + width], preferred_element_type=F32)

    assert B_WIDTH == C_WIDTH and (2 * B_WIDTH) % MXU_COLUMNS == 0
    pa = proj(0, 2 * A_WIDTH)
    c0 = 2 * A_WIDTH
    qk = proj(c0, 2 * B_WIDTH)
    vq = proj(c0 + 2 * B_WIDTH, 2 * B_WIDTH)
    kv = proj(c0 + 4 * B_WIDTH, 2 * B_WIDTH)
    u = jax.nn.gelu(pa[:, :A_WIDTH])
    v = jax.nn.gelu(pa[:, A_WIDTH:])
    grp_r = lax.broadcasted_iota(jnp.int32, (A_WIDTH, A_WIDTH), 0) // HEAD_DIM
    grp_c = lax.broadcasted_iota(jnp.int32, (A_WIDTH, A_WIDTH), 1) // HEAD_DIM
    group_ones = jnp.where(grp_r == grp_c, 1.0, 0.0).astype(BF16)
    hi, lo = _split_bf16(v * v)
    ssq = (jnp.dot(hi, group_ones, preferred_element_type=F32)
           + jnp.dot(lo, group_ones, preferred_element_type=F32))
    vn = v * lax.rsqrt(ssq * (1.0 / HEAD_DIM) + EPS) * gsgu_ref[...]
    t_idx = lax.broadcasted_iota(jnp.int32, (CHUNK, A_GROUPS * CHUNK), 0)
    s_idx = lax.broadcasted_iota(jnp.int32, (CHUNK, A_GROUPS * CHUNK), 1) % CHUNK
    w_causal = jnp.where(s_idx <= t_idx, wsp_ref[...], 0.0).astype(BF16)
    lane_grp = lax.broadcasted_iota(jnp.int32, (CHUNK, A_WIDTH), 1) // HEAD_DIM
    for c in range(tm // CHUNK):
        rows = slice(c * CHUNK, (c + 1) * CHUNK)
        vc = vn[rows]
        stacked = jnp.concatenate(
            [jnp.where(lane_grp == g, vc, 0.0) for g in range(A_GROUPS)], axis=0).astype(BF16)
        mixed = jnp.dot(w_causal, stacked, preferred_element_type=F32) + bsp_ref[...]
        ya_ref[rows, :] = (u[rows] * mixed).astype(BF16)

    cos = cos_ref[...]
    sin = sin_ref[...]
    first_half = (lax.broadcasted_iota(jnp.int32, (tm, LANES), 1) % HEAD_DIM) < HEAD_DIM // 2

    def rope(pc):
        outs = []
        for p in range(C_WIDTH // LANES):
            xp = pc[:, p * LANES:(p + 1) * LANES]
            rot = jnp.where(first_half,
                            pltpu.roll(xp, LANES - HEAD_DIM // 2, axis=1),
                            pltpu.roll(xp, HEAD_DIM // 2, axis=1))
            outs.append(xp * cos + rot * sin)
        return jnp.concatenate(outs, axis=1)

    bq_ref[...] = (qk[:, :B_WIDTH] * (scale * LOG2_E)).astype(BF16)
    bk_ref[...] = qk[:, B_WIDTH:].astype(BF16)
    bv_ref[...] = vq[:, :B_WIDTH].astype(BF16)
    cq_ref[...] = rope(vq[:, B_WIDTH:]) * scale
    ck_ref[...] = rope(kv[:, :C_WIDTH])
    cv_ref[...] = kv[:, C_WIDTH:]


def _in_projection(x2d, mod_l, g_mix, w_in, cos, sin, g_sgu, w_sp, b_sp, to_cast, *, layer, seq, tm):
    tokens, d = x2d.shape
    per_batch = seq // tm
    in_width = w_in.shape[-1]
    w_cat = jnp.transpose(w_sp, (1, 0, 2)).reshape(CHUNK, A_GROUPS * CHUNK)
    bias = jnp.repeat(b_sp.T, HEAD_DIM, axis=1)
    tile = lambda w: pl.BlockSpec((tm, w), lambda i: (i, 0))
    full = lambda a: pl.BlockSpec(a.shape, lambda i: (0,) * a.ndim)
    g_mix = g_mix.reshape(1, d)
    g_sgu = g_sgu.reshape(1, A_WIDTH)
    steps = tokens // tm
    casts = [_cast_specs(w, layer, steps) for w in to_cast]
    outs = pl.pallas_call(
        functools.partial(_inproj_kernel, tm=tm, n_cast=len(casts)),
        grid=(steps,),
        in_specs=[tile(d),
                  pl.BlockSpec((1, N_MOD, d), lambda i: (i // per_batch, 0, 0)),
                  full(g_mix), _resident(w_in), tile(LANES), tile(LANES),
                  full(g_sgu), full(w_cat), full(bias)] + [c[0] for c in casts],
        out_specs=[tile(A_WIDTH)] + [tile(B_WIDTH)] * 3 + [tile(C_WIDTH)] * 3 + [c[1] for c in casts],
        out_shape=[jax.ShapeDtypeStruct((tokens, w), dt) for w, dt in
                   [(A_WIDTH, BF16)] + [(B_WIDTH, BF16)] * 3 + [(C_WIDTH, F32)] * 3]
        + [c[2] for c in casts],
        scratch_shapes=[pltpu.VMEM((tm, d), BF16)],
        compiler_params=_params("arbitrary"),
        name="in_projection",
    )(x2d, mod_l, g_mix, w_in, cos, sin, g_sgu, w_cat, bias, *to_cast)
    assert in_width == 2 * A_WIDTH + 3 * B_WIDTH + 3 * C_WIDTH
    return outs[:7], outs[7:]


def _stick_kernel(q_ref, k_ref, v_ref, later_ref, o_ref, acc_scr, carry_scr, top_scr, *, blk, lead,
                  n_q):
    first_q = pl.program_id(2) * n_q
    rows = HEADS_PER_TILE * blk
    lane_head = lax.broadcasted_iota(jnp.int32, (blk, LANES), 1) // HEAD_DIM
    t_idx = lax.broadcasted_iota(jnp.int32, (rows, blk), 0) % blk
    before = lax.broadcasted_iota(jnp.int32, (rows, blk), 1) < t_idx

    def stacked_q(j):
        q = q_ref[j * blk:(j + 1) * blk, :]
        return jnp.concatenate(
            [jnp.where(lane_head == h, q, jnp.zeros_like(q)) for h in range(HEADS_PER_TILE)], axis=0)

    def visit(q_stack, first_block, n_blocks, diagonal, carry, acc, out):
        start = pl.multiple_of(first_block * blk, blk)
        k_win = k_ref[pl.ds(start, n_blocks * blk), :]
        v_win = v_ref[pl.ds(start, n_blocks * blk), :]
        z = lax.dot_general(q_stack, k_win, (((1,), (1,)), ((), ())), preferred_element_type=F32)
        yield
        log_beta = jnp.minimum(z, 0.0) - jnp.log2(1.0 + jnp.exp2(-jnp.abs(z)))
        log_stay = log_beta - z
        parts = []
        for w in range(n_blocks):
            stay_w = log_stay[:, w * blk:(w + 1) * blk]
            if diagonal and w == n_blocks - 1:
                stay_w = jnp.where(before, stay_w, 0.0)
            parts.append(jnp.concatenate(_split_bf16(stay_w), axis=1))
        sums = jnp.dot(jnp.concatenate(parts, axis=0), later_ref[...], preferred_element_type=F32)
        yield
        weights = [None] * n_blocks
        for w in reversed(range(n_blocks)):
            sums_w = sums[w * rows:(w + 1) * rows]
            weight = jnp.exp2(log_beta[:, w * blk:(w + 1) * blk] + sums_w[:, :blk] + carry)
            if diagonal and w == n_blocks - 1:
                weight = jnp.where(before, weight, 0.0)
            weights[w] = weight.astype(BF16)
            carry = carry + sums_w[:, blk:]
        acc = acc + jnp.dot(jnp.concatenate(weights, axis=1), v_win, preferred_element_type=F32)
        out.append((carry, acc))

    def run_staged(visits):
        while visits:
            visits = [g for g in visits if next(g, True) is None]

    def first_visits(first_q, n_blocks):
        zero = jnp.zeros((rows, blk), F32)
        outs = [[] for _ in range(n_q)]
        run_staged([visit(stacked_q(j), first_q + j - (n_blocks[j] - 1), n_blocks[j], True,
                          zero, zero, outs[j]) for j in range(n_q)])
        for j in range(n_q):
            carry_scr[j], acc_scr[j] = outs[j][0]
        tops = [jnp.max(outs[j][0][0], axis=0, keepdims=True)
                for j in range(n_q) if n_blocks[j] == lead]
        top = functools.reduce(jnp.maximum, tops) if tops else jnp.full((1, blk), -jnp.inf, F32)
        top_scr[...] = jnp.broadcast_to(top, top_scr.shape)

    @pl.when(first_q == 0)
    def _():
        first_visits(0, [min(j + 1, lead) for j in range(n_q)])

    @pl.when(first_q > 0)
    def _():
        first_visits(first_q, [lead] * n_q)

    def alive(carry):
        top = jnp.max(carry, axis=0, keepdims=True)
        return (top[0, 0] > EXP2_ZERO_BELOW).astype(jnp.int32)

    def finish(j):
        def cond(state):
            kb, go = state
            return jnp.logical_and(kb >= 0, go > 0)

        def body(state):
            kb, _ = state
            out = []
            run_staged([visit(stacked_q(j), kb, 1, False, carry_scr[j], acc_scr[j], out)])
            carry_scr[j], acc_scr[j] = out[0]
            return kb - 1, alive(carry_scr[j])

        lax.while_loop(cond, body, (first_q + j - lead, alive(carry_scr[j])))

    @pl.when(alive(top_scr[...]) > 0)
    def _():
        for j in range(n_q):
            finish(j)

    for j in range(n_q):
        o_ref[j * blk:(j + 1) * blk, :] = jnp.where(
            lane_head == 0, acc_scr[j, :blk], acc_scr[j, blk:]).astype(BF16)


def _stick_breaking(q, k, v, *, bsz, seq):
    tokens, width = q.shape
    blk = Q_BLOCK
    n_q = STICK_Q_BLOCKS_PER_STEP
    assert n_q >= STICK_LEAD_BLOCKS - 1
    steps = seq // (blk * n_q)
    j_idx = lax.broadcasted_iota(jnp.int32, (2 * blk, 2 * blk), 0) % blk
    s_idx = lax.broadcasted_iota(jnp.int32, (2 * blk, 2 * blk), 1)
    later = jnp.where((j_idx > s_idx) | (s_idx >= blk), 1.0, 0.0).astype(BF16)
    q_spec = pl.BlockSpec((n_q * blk, LANES), lambda b, p, i: (b * steps + i, p))
    kv_spec = pl.BlockSpec((seq, LANES), lambda b, p, i: (b, p))
    return pl.pallas_call(
        functools.partial(_stick_kernel, blk=blk, lead=STICK_LEAD_BLOCKS, n_q=n_q),
        grid=(bsz, width // LANES, steps),
        in_specs=[q_spec, kv_spec, kv_spec, pl.BlockSpec(later.shape, lambda b, p, i: (0, 0))],
        out_specs=q_spec,
        out_shape=jax.ShapeDtypeStruct((tokens, width), BF16),
        scratch_shapes=[pltpu.VMEM((n_q, HEADS_PER_TILE * blk, LANES), F32),
                        pltpu.VMEM((n_q, HEADS_PER_TILE * blk, blk), F32),
                        pltpu.VMEM((8, blk), F32)],
        compiler_params=_params("parallel", "parallel", "parallel"),
        name="stick_breaking",
    )(q, k, v, later)


def _window_kernel(q_ref, k_ref, v_ref, o_ref, lse_ref, *, seq, max_tq):
    pair = pl.program_id(1)
    rows = HEADS_PER_TILE * WINDOW
    lane_head = lax.broadcasted_iota(jnp.int32, (WINDOW, LANES), 1) // HEAD_DIM
    off = (lax.broadcasted_iota(jnp.int32, (rows, 2 * WINDOW), 1)
           - lax.broadcasted_iota(jnp.int32, (rows, 2 * WINDOW), 0) % WINDOW)
    band = (off >= 0) & (off <= WINDOW)
    from_prev = lax.broadcasted_iota(jnp.int32, (rows, 2 * WINDOW), 1) < WINDOW

    def class_rows(first, d):
        return pl.ds(first, WINDOW) if d == 1 else pl.ds(first, WINDOW, stride=d)

    def block(d, r, j, prev, out):
        cur = class_rows(r + d * j, d)
        q = q_ref[cur, :].astype(BF16)
        k_cur = k_ref[cur, :].astype(BF16)
        v_cur = v_ref[cur, :].astype(BF16)
        out.append((k_cur, v_cur))
        if prev is None:
            before = class_rows(r + d * jnp.maximum(j - WINDOW, 0), d)
            prev = (k_ref[before, :].astype(BF16), v_ref[before, :].astype(BF16))
            valid = band & (jnp.logical_not(from_prev) | (j > 0))
        else:
            valid = band
        q_stack = jnp.concatenate(
            [jnp.where(lane_head == h, q, jnp.zeros_like(q)) for h in range(HEADS_PER_TILE)], axis=0)
        z = lax.dot_general(q_stack, jnp.concatenate([prev[0], k_cur], axis=0),
                            (((1,), (1,)), ((), ())), preferred_element_type=F32)
        yield
        z = jnp.where(valid, z, -jnp.inf)
        z_max = jnp.max(z, axis=-1, keepdims=True)
        prob = jnp.exp(z - z_max)
        denom = jnp.sum(prob, axis=-1, keepdims=True)
        pv = jnp.dot(prob.astype(BF16), jnp.concatenate([prev[1], v_cur], axis=0),
                     preferred_element_type=F32)
        yield
        o = pv * (1.0 / denom)
        lse = z_max + jnp.log(denom)
        o_ref[cur, :] = jnp.where(lane_head == 0, o[:WINDOW], o[WINDOW:])
        lse_ref[cur, :] = jnp.where(lane_head == 0, lse[:WINDOW], lse[WINDOW:])

    def tiles(d):
        tq = min(max_tq, seq // d)
        runs = max_tq // tq
        per_class = seq // d // tq

        def tile(i, _):
            visits = []
            for c in range(runs):
                run = i * runs + c
                r = run // per_class
                j0 = (run % per_class) * tq
                prev = None
                for s in range(tq // WINDOW):
                    out = []
                    visits.append(block(d, r, j0 + s * WINDOW, prev, out))
                    next(visits[-1])
                    prev = out[0]
            while visits:
                visits = [g for g in visits if next(g, True) is None]
            return 0

        lax.fori_loop(0, seq // max_tq, tile, 0)

    for idx, d in enumerate(DILATIONS):
        pl.when(pair == idx)(functools.partial(tiles, d))


def _dilated_window(q, k, v, *, bsz, seq):
    tokens, width = q.shape
    spec = pl.BlockSpec((seq, LANES), lambda b, p: (b, p))
    return pl.pallas_call(
        functools.partial(_window_kernel, seq=seq, max_tq=WINDOW_TILE_ROWS),
        grid=(bsz, width // LANES),
        in_specs=[spec] * 3,
        out_specs=[spec] * 2,
        out_shape=[jax.ShapeDtypeStruct((tokens, width), F32)] * 2,
        compiler_params=_params("parallel", "parallel"),
        name="dilated_window",
    )(q, k, v)


def _layer_tail_kernel(x_ref, mod_ref, ya_ref, yb_ref, oc_ref, lse_ref, wout_ref, g_ref, wup_ref,
                       cw_ref, cb_ref, wdn_ref, gfin_ref, *rest, tm, tn, per_batch, final):
    if final:
        o_ref, y_scr, x_scr, tail_scr, h_scr, up_scr, act_scr, down_scr = rest
    else:
        win_ref, o_ref, win_bf_ref, y_scr, x_scr, tail_scr, h_scr, up_scr, act_scr, down_scr = rest
        win_bf_ref[...] = win_ref[...].astype(BF16)
    i = pl.program_id(0)
    mod = mod_ref[0]
    d_ff = wdn_ref.shape[0]

    y_scr[:, :A_WIDTH] = ya_ref[...]
    y_scr[:, A_WIDTH:A_WIDTH + B_WIDTH] = yb_ref[...]
    pairs = C_WIDTH // LANES
    lse = [lse_ref[:, p * LANES:(p + 1) * LANES] for p in range(pairs)]
    top = functools.reduce(jnp.maximum, lse)
    share = [jnp.exp(l - top) for l in lse]
    inv = 1.0 / functools.reduce(lambda a, b: a + b, share)
    c0 = A_WIDTH + B_WIDTH
    for p in range(pairs):
        o_p = oc_ref[:, p * LANES:(p + 1) * LANES]
        y_scr[:, c0 + p * LANES:c0 + (p + 1) * LANES] = (o_p * (share[p] * inv)).astype(BF16)
    mixed = jnp.dot(y_scr[...], wout_ref[...], preferred_element_type=F32)
    x_scr[...] = x_ref[...] + (1.0 + mod[2:3]) * mixed

    def norm_mod(xv):
        return _rms_scale(xv) * g_ref[...] * (1.0 + mod[4:5]) + mod[3:4]

    opens_sequence = i % per_batch == 0

    @pl.when(opens_sequence)
    def _():
        tail_scr[...] = jnp.zeros_like(tail_scr)

    halo = jnp.where(opens_sequence, 0.0, norm_mod(tail_scr[...]))
    h_scr[:CONV_HALO, :] = halo.astype(BF16)
    h_scr[CONV_HALO:, :] = norm_mod(x_scr[...]).astype(BF16)
    tail_scr[...] = x_scr[tm - CONV_HALO:, :]

    half = tm // 2

    slabs = range(tn // LANES)
    n_tiles = d_ff // tn
    in_flight = up_scr.shape[0] // 2

    def project(n):
        for part in range(2):
            c0 = part * d_ff + n * tn
            up = jnp.dot(h_scr[...], wup_ref[:, c0:c0 + tn], preferred_element_type=F32)
            for s in slabs:
                up_scr[2 * (n % in_flight) + part, s] = up[:, s * LANES:(s + 1) * LANES]

    def conv(slot, c0):
        w = cw_ref[:, c0:c0 + tn]
        bias = cb_ref[:, c0:c0 + tn]
        parities = []
        for parity in range(2):
            cols = []
            for s in slabs:
                lanes = slice(s * LANES, (s + 1) * LANES)
                out = bias[:, lanes]
                for tap in range(CONV_WIDTH):
                    first = CONV_HALO - (CONV_WIDTH - 1) + tap + parity
                    out = out + w[tap:tap + 1, lanes] * up_scr[slot, s, pl.ds(first, half, stride=2), :]
                cols.append(out)
            parities.append(jnp.concatenate(cols, axis=1))
        return jnp.concatenate(parities, axis=0)

    project(0)
    for n in range(n_tiles):
        if n + 1 < n_tiles:
            project(n + 1)
        slot = 2 * (n % in_flight)
        gate = conv(slot, n * tn)
        val = conv(slot + 1, d_ff + n * tn)
        act = gate * (1.0 / (1.0 + jnp.exp(-gate))) * val
        act_scr[:, n * tn:(n + 1) * tn] = act.astype(BF16)
    down = jnp.dot(act_scr[...], wdn_ref[...], preferred_element_type=F32)
    for s in range(down_scr.shape[0]):
        lanes = slice(s * LANES, (s + 1) * LANES)
        down_scr[s, pl.ds(0, half, stride=2), :] = down[:half, lanes]
        down_scr[s, pl.ds(1, half, stride=2), :] = down[half:, lanes]
    down = jnp.concatenate([down_scr[s] for s in range(down_scr.shape[0])], axis=1)
    x_new = x_scr[...] + (1.0 + mod[5:6]) * down
    if final:
        x_new = _rms_scale(x_new) * gfin_ref[...]
    o_ref[...] = x_new


def _layer_tail(x2d, mod_l, y_a, y_b, o_c, lse, w_out, g_ffn, w_up, conv_w, conv_b, w_down, g_final,
                w_in_all, *, layer, seq, tm, tn, final):
    tokens, d = x2d.shape
    per_batch = seq // tm
    tile = lambda w: pl.BlockSpec((tm, w), lambda i: (i, 0))
    full = lambda a: pl.BlockSpec(a.shape, lambda i: (0,) * a.ndim)
    g_ffn = g_ffn.reshape(1, d)
    conv_b = conv_b.reshape(1, -1)
    g_final = g_final.reshape(1, d)
    steps = tokens // tm
    cast = [] if final else [_cast_specs(w_in_all, layer + 1, steps)]
    outs = pl.pallas_call(
        functools.partial(_layer_tail_kernel, tm=tm, tn=tn, per_batch=per_batch, final=final),
        grid=(steps,),
        in_specs=[tile(d),
                  pl.BlockSpec((1, N_MOD, d), lambda i: (i // per_batch, 0, 0)),
                  tile(A_WIDTH), tile(B_WIDTH), tile(C_WIDTH), tile(C_WIDTH), _resident(w_out),
                  full(g_ffn), _resident(w_up), full(conv_w), full(conv_b), _resident(w_down),
                  full(g_final)] + [c[0] for c in cast],
        out_specs=[tile(d)] + [c[1] for c in cast],
        out_shape=[jax.ShapeDtypeStruct((tokens, d), F32)] + [c[2] for c in cast],
        scratch_shapes=[pltpu.VMEM((tm, w_out.shape[0]), BF16),
                        pltpu.VMEM((tm, d), F32),
                        pltpu.VMEM((CONV_HALO, d), F32),
                        pltpu.VMEM((tm + CONV_HALO, d), BF16),
                        pltpu.VMEM((4, tn // LANES, tm + CONV_HALO, LANES), F32),
                        pltpu.VMEM((tm, w_down.shape[0]), BF16),
                        pltpu.VMEM((d // LANES, tm, LANES), F32)],
        compiler_params=_params("arbitrary"),
        name="layer_tail",
    )(x2d, mod_l, y_a, y_b, o_c, lse, w_out, g_ffn, w_up, conv_w, conv_b, w_down, g_final,
      *([] if final else [w_in_all]))
    return outs[0], (None if final else outs[1])


def kernel(x, c, positions, w_ada, b_ada, g_mix, w_in, g_sgu, w_sp, b_sp, w_out, g_ffn, w_up,
           conv_w, conv_b, w_down, g_final):
    bsz, seq, d = x.shape
    depth = w_ada.shape[0]
    tm = min(512, seq)
    assert seq % tm == 0 and seq % (max(DILATIONS) * WINDOW) == 0
    mod = _modulation(c, w_ada, b_ada)
    cos, sin = _rope_tables(positions)
    cos = cos.reshape(bsz * seq, LANES)
    sin = sin.reshape(bsz * seq, LANES)
    x2d = x.reshape(bsz * seq, d)
    w_in_l = w_in[0].astype(BF16)
    for l in range(depth):
        (y_a, b_q, b_k, b_v, c_q, c_k, c_v), (w_out_l, w_up_l, w_down_l) = _in_projection(
            x2d, mod[l], g_mix[l], w_in_l, cos, sin, g_sgu[l], w_sp[l], b_sp[l],
            (w_out, w_up, w_down), layer=l, seq=seq, tm=min(1024, seq))
        y_b = _stick_breaking(b_q, b_k, b_v, bsz=bsz, seq=seq)
        o_c, lse = _dilated_window(c_q, c_k, c_v, bsz=bsz, seq=seq)
        x2d, w_in_l = _layer_tail(x2d, mod[l], y_a, y_b, o_c, lse, w_out_l, g_ffn[l], w_up_l,
                                  conv_w[l], conv_b[l], w_down_l, g_final, w_in, layer=l, seq=seq,
                                  tm=tm, tn=256, final=(l == depth - 1))
    return x2d.reshape(bsz, seq, d)
```

```python
import functools

import jax
import jax.numpy as jnp
from jax import lax
from jax.experimental import pallas as pl
from jax.experimental.pallas import tpu as pltpu

F32 = jnp.float32
BF16 = jnp.bfloat16

LANES = 128
BF16_ROWS = 16
MXU_COLUMNS = 256
HEAD_DIM = 64
HEADS_PER_TILE = LANES // HEAD_DIM
A_GROUPS = 4
A_WIDTH = A_GROUPS * HEAD_DIM
CHUNK = 128
B_WIDTH = 6 * HEAD_DIM
C_WIDTH = 6 * HEAD_DIM
DILATIONS = (1, 4, 16)
WINDOW = 128
WINDOW_TILE_ROWS = 512
Q_BLOCK = 128
STICK_LEAD_BLOCKS = 3
STICK_Q_BLOCKS_PER_STEP = 16
CONV_WIDTH = 3
CONV_HALO = 8
ROPE_THETA = 10000.0
EPS = 1e-6
N_MOD = 6
LOG2_E = 1.4426950408889634
EXP2_ZERO_BELOW = -151.0
VMEM_LIMIT_BYTES = 56 * 1024 * 1024


def _params(*semantics):
    return pltpu.CompilerParams(dimension_semantics=semantics,
                                vmem_limit_bytes=VMEM_LIMIT_BYTES)


def _resident(w):
    return pl.BlockSpec(w.shape, lambda i: (0, 0), pipeline_mode=pl.Buffered(1))


def _cast_specs(stacked, layer, steps):
    _, rows, cols = stacked.shape
    repeat = next(g for g in (1, 2, 4, 8, 16) if rows * g % (steps * BF16_ROWS) == 0)
    slab = rows * repeat // steps
    return (pl.BlockSpec((None, slab, cols), lambda i: (layer, i // repeat, 0)),
            pl.BlockSpec((slab, cols), lambda i: (i // repeat, 0)),
            jax.ShapeDtypeStruct((rows, cols), BF16))


def _pair_major_tile(tm):
    return pl.BlockSpec((C_WIDTH // LANES, tm, LANES), lambda i: (0, i, 0))


def _split_bf16(a):
    hi = a.astype(BF16)
    lo = (a - hi.astype(F32)).astype(BF16)
    return hi, lo


def _rms_scale(x):
    return x * lax.rsqrt(jnp.mean(x * x, axis=-1, keepdims=True) + EPS)


def _mod_kernel(c_ref, w_ref, b_ref, o_ref):
    c = c_ref[...]
    c_act = c * (1.0 / (1.0 + jnp.exp(-c)))
    o_ref[0] = jnp.dot(c_act, w_ref[0], preferred_element_type=F32) + b_ref[0]


def _modulation(c, w_ada, b_ada):
    depth, d, nd = w_ada.shape
    bsz = c.shape[0]
    rows = -(-bsz // 8) * 8
    c_pad = jnp.zeros((rows, d), F32).at[:bsz].set(c)
    out = pl.pallas_call(
        _mod_kernel,
        grid=(depth, nd // d),
        in_specs=[pl.BlockSpec((rows, d), lambda l, j: (0, 0)),
                  pl.BlockSpec((1, d, d), lambda l, j: (l, 0, j)),
                  pl.BlockSpec((1, 1, d), lambda l, j: (l, 0, j))],
        out_specs=pl.BlockSpec((1, rows, d), lambda l, j: (l, 0, j)),
        out_shape=jax.ShapeDtypeStruct((depth, rows, nd), F32),
        compiler_params=_params("parallel", "parallel"),
        name="adaln_modulation",
    )(c_pad, w_ada, b_ada.reshape(depth, 1, nd))
    return out[:, :bsz].reshape(depth, bsz, N_MOD, d)


def _rope_kernel(pos_ref, freq_ref, sign_ref, cos_ref, sin_ref):
    n_freq = HEAD_DIM // 2
    per_row = LANES // n_freq
    pos = pos_ref[0]
    rows = pos.shape[0]
    lane_pos = lax.broadcasted_iota(jnp.int32, (rows, LANES), 1) // n_freq
    pos_lanes = jnp.zeros((rows, LANES), F32)
    for g in range(per_row):
        pos_lanes = jnp.where(lane_pos == g, pos[:, g:g + 1], pos_lanes)
    ang = pos_lanes * freq_ref[...]
    for table, out_ref, scale in ((jnp.cos(ang), cos_ref, None), (jnp.sin(ang), sin_ref, sign_ref[...])):
        for g in range(per_row):
            picked = jnp.where(lane_pos == g, table, 0.0)
            spread = picked
            for k in range(1, per_row):
                spread = spread + pltpu.roll(picked, k * n_freq, axis=1)
            if scale is not None:
                spread = spread * scale
            out_ref[0, pl.ds(g, rows, stride=per_row), :] = spread


def _rope_tables(positions):
    bsz, seq = positions.shape
    half = HEAD_DIM // 2
    per_row = LANES // half
    inv_freq = ROPE_THETA ** (-jnp.arange(0, HEAD_DIM, 2, dtype=F32) / HEAD_DIM)
    freq = jnp.tile(inv_freq, per_row).reshape(1, LANES)
    sign = jnp.tile(jnp.concatenate([-jnp.ones((half,), F32), jnp.ones((half,), F32)]),
                    HEADS_PER_TILE).reshape(1, LANES)
    pos = positions.astype(F32).reshape(bsz, seq // per_row, per_row)
    row = pl.BlockSpec((1, LANES), lambda b: (0, 0))
    tab = pl.BlockSpec((1, seq, LANES), lambda b: (b, 0, 0))
    return pl.pallas_call(
        _rope_kernel,
        grid=(bsz,),
        in_specs=[pl.BlockSpec((1, seq // per_row, per_row), lambda b: (b, 0, 0)), row, row],
        out_specs=[tab, tab],
        out_shape=[jax.ShapeDtypeStruct((bsz, seq, LANES), F32)] * 2,
        compiler_params=_params("parallel"),
        name="rope_tables",
    )(pos, freq, sign)


def _inproj_kernel(x_ref, mod_ref, g_ref, w_ref, cos_ref, sin_ref, gsgu_ref, wsp_ref, bsp_ref,
                   *rest, tm, n_cast):
    cast_from, rest = rest[:n_cast], rest[n_cast:]
    ya_ref, bq_ref, bk_ref, bv_ref, cq_ref, ck_ref, cv_ref = rest[:7]
    cast_to, (h_scr,) = rest[7:7 + n_cast], rest[7 + n_cast:]
    for src_ref, dst_ref in zip(cast_from, cast_to):
        dst_ref[...] = src_ref[...].astype(BF16)
    mod = mod_ref[0]
    h = _rms_scale(x_ref[...]) * g_ref[...] * (1.0 + mod[1:2]) + mod[0:1]
    h_scr[...] = h.astype(BF16)
    scale = HEAD_DIM ** -0.5

    def proj(c0, width):
        return jnp.dot(h_scr[...], w_ref[:, c0:c0 + width], preferred_element_type=F32)

    assert B_WIDTH == C_WIDTH and (2 * B_WIDTH) % MXU_COLUMNS == 0
    pa = proj(0, 2 * A_WIDTH)
    c0 = 2 * A_WIDTH
    qk = proj(c0, 2 * B_WIDTH)
    vq = proj(c0 + 2 * B_WIDTH, 2 * B_WIDTH)
    kv = proj(c0 + 4 * B_WIDTH, 2 * B_WIDTH)
    u = jax.nn.gelu(pa[:, :A_WIDTH])
    v = jax.nn.gelu(pa[:, A_WIDTH:])
    grp_r = lax.broadcasted_iota(jnp.int32, (A_WIDTH, A_WIDTH), 0) // HEAD_DIM
    grp_c = lax.broadcasted_iota(jnp.int32, (A_WIDTH, A_WIDTH), 1) // HEAD_DIM
    group_ones = jnp.where(grp_r == grp_c, 1.0, 0.0).astype(BF16)
    hi, lo = _split_bf16(v * v)
    ssq = (jnp.dot(hi, group_ones, preferred_element_type=F32)
           + jnp.dot(lo, group_ones, preferred_element_type=F32))
    vn = v * lax.rsqrt(ssq * (1.0 / HEAD_DIM) + EPS) * gsgu_ref[...]
    t_idx = lax.broadcasted_iota(jnp.int32, (CHUNK, A_GROUPS * CHUNK), 0)
    s_idx = lax.broadcasted_iota(jnp.int32, (CHUNK, A_GROUPS * CHUNK), 1) % CHUNK
    w_causal = jnp.where(s_idx <= t_idx, wsp_ref[...], 0.0).astype(BF16)
    lane_grp = lax.broadcasted_iota(jnp.int32, (CHUNK, A_WIDTH), 1) // HEAD_DIM
    for c in range(tm // CHUNK):
        rows = slice(c * CHUNK, (c + 1) * CHUNK)
        vc = vn[rows]
        stacked = jnp.concatenate(
            [jnp.where(lane_grp == g, vc, 0.0) for g in range(A_GROUPS)], axis=0).astype(BF16)
        mixed = jnp.dot(w_causal, stacked, preferred_element_type=F32) + bsp_ref[...]
        ya_ref[rows, :] = (u[rows] * mixed).astype(BF16)

    cos = cos_ref[...]
    sin = sin_ref[...]
    first_half = (lax.broadcasted_iota(jnp.int32, (tm, LANES), 1) % HEAD_DIM) < HEAD_DIM // 2

    def rope(xp):
        rot = jnp.where(first_half,
                        pltpu.roll(xp, LANES - HEAD_DIM // 2, axis=1),
                        pltpu.roll(xp, HEAD_DIM // 2, axis=1))
        return xp * cos + rot * sin

    bq_ref[...] = (qk[:, :B_WIDTH] * (scale * LOG2_E)).astype(BF16)
    bk_ref[...] = qk[:, B_WIDTH:].astype(BF16)
    bv_ref[...] = vq[:, :B_WIDTH].astype(BF16)
    for p in range(C_WIDTH // LANES):
        lanes = slice(p * LANES, (p + 1) * LANES)
        cq_ref[p] = rope(vq[:, B_WIDTH:][:, lanes]) * scale
        ck_ref[p] = rope(kv[:, :C_WIDTH][:, lanes])
        cv_ref[p] = kv[:, C_WIDTH:][:, lanes]


def _in_projection(x2d, mod_l, g_mix, w_in, cos, sin, g_sgu, w_sp, b_sp, to_cast, *, layer, seq, tm):
    tokens, d = x2d.shape
    per_batch = seq // tm
    in_width = w_in.shape[-1]
    w_cat = jnp.transpose(w_sp, (1, 0, 2)).reshape(CHUNK, A_GROUPS * CHUNK)
    bias = jnp.repeat(b_sp.T, HEAD_DIM, axis=1)
    tile = lambda w: pl.BlockSpec((tm, w), lambda i: (i, 0))
    full = lambda a: pl.BlockSpec(a.shape, lambda i: (0,) * a.ndim)
    g_mix = g_mix.reshape(1, d)
    g_sgu = g_sgu.reshape(1, A_WIDTH)
    steps = tokens // tm
    casts = [_cast_specs(w, layer, steps) for w in to_cast]
    outs = pl.pallas_call(
        functools.partial(_inproj_kernel, tm=tm, n_cast=len(casts)),
        grid=(steps,),
        in_specs=[tile(d),
                  pl.BlockSpec((1, N_MOD, d), lambda i: (i // per_batch, 0, 0)),
                  full(g_mix), _resident(w_in), tile(LANES), tile(LANES),
                  full(g_sgu), full(w_cat), full(bias)] + [c[0] for c in casts],
        out_specs=[tile(A_WIDTH)] + [tile(B_WIDTH)] * 3 + [_pair_major_tile(tm)] * 3
        + [c[1] for c in casts],
        out_shape=[jax.ShapeDtypeStruct((tokens, w), BF16) for w in (A_WIDTH,) + (B_WIDTH,) * 3]
        + [jax.ShapeDtypeStruct((C_WIDTH // LANES, tokens, LANES), F32)] * 3
        + [c[2] for c in casts],
        scratch_shapes=[pltpu.VMEM((tm, d), BF16)],
        compiler_params=_params("arbitrary"),
        name="in_projection",
    )(x2d, mod_l, g_mix, w_in, cos, sin, g_sgu, w_cat, bias, *to_cast)
    assert in_width == 2 * A_WIDTH + 3 * B_WIDTH + 3 * C_WIDTH
    return outs[:7], outs[7:]


def _stick_kernel(q_ref, k_ref, v_ref, later_ref, o_ref, acc_scr, carry_scr, top_scr, *, blk, lead,
                  n_q):
    first_q = pl.program_id(2) * n_q
    rows = HEADS_PER_TILE * blk
    lane_head = lax.broadcasted_iota(jnp.int32, (blk, LANES), 1) // HEAD_DIM
    t_idx = lax.broadcasted_iota(jnp.int32, (rows, blk), 0) % blk
    before = lax.broadcasted_iota(jnp.int32, (rows, blk), 1) < t_idx

    def stacked_q(j):
        q = q_ref[j * blk:(j + 1) * blk, :]
        return jnp.concatenate(
            [jnp.where(lane_head == h, q, jnp.zeros_like(q)) for h in range(HEADS_PER_TILE)], axis=0)

    def visit(q_stack, first_block, n_blocks, diagonal, carry, acc, out):
        start = pl.multiple_of(first_block * blk, blk)
        k_win = k_ref[pl.ds(start, n_blocks * blk), :]
        v_win = v_ref[pl.ds(start, n_blocks * blk), :]
        z = lax.dot_general(q_stack, k_win, (((1,), (1,)), ((), ())), preferred_element_type=F32)
        yield
        log_beta = jnp.minimum(z, 0.0) - jnp.log2(1.0 + jnp.exp2(-jnp.abs(z)))
        log_stay = log_beta - z
        parts = []
        for w in range(n_blocks):
            stay_w = log_stay[:, w * blk:(w + 1) * blk]
            if diagonal and w == n_blocks - 1:
                stay_w = jnp.where(before, stay_w, 0.0)
            parts.append(jnp.concatenate(_split_bf16(stay_w), axis=1))
        sums = jnp.dot(jnp.concatenate(parts, axis=0), later_ref[...], preferred_element_type=F32)
        yield
        weights = [None] * n_blocks
        for w in reversed(range(n_blocks)):
            sums_w = sums[w * rows:(w + 1) * rows]
            weight = jnp.exp2(log_beta[:, w * blk:(w + 1) * blk] + sums_w[:, :blk] + carry)
            if diagonal and w == n_blocks - 1:
                weight = jnp.where(before, weight, 0.0)
            weights[w] = weight.astype(BF16)
            carry = carry + sums_w[:, blk:]
        acc = acc + jnp.dot(jnp.concatenate(weights, axis=1), v_win, preferred_element_type=F32)
        out.append((carry, acc))

    def run_staged(visits):
        while visits:
            visits = [g for g in visits if next(g, True) is None]

    def first_visits(first_q, n_blocks):
        zero = jnp.zeros((rows, blk), F32)
        outs = [[] for _ in range(n_q)]
        run_staged([visit(stacked_q(j), first_q + j - (n_blocks[j] - 1), n_blocks[j], True,
                          zero, zero, outs[j]) for j in range(n_q)])
        for j in range(n_q):
            carry_scr[j], acc_scr[j] = outs[j][0]
        tops = [jnp.max(outs[j][0][0], axis=0, keepdims=True)
                for j in range(n_q) if n_blocks[j] == lead]
        top = functools.reduce(jnp.maximum, tops) if tops else jnp.full((1, blk), -jnp.inf, F32)
        top_scr[...] = jnp.broadcast_to(top, top_scr.shape)

    @pl.when(first_q == 0)
    def _():
        first_visits(0, [min(j + 1, lead) for j in range(n_q)])

    @pl.when(first_q > 0)
    def _():
        first_visits(first_q, [lead] * n_q)

    def alive(carry):
        top = jnp.max(carry, axis=0, keepdims=True)
        return (top[0, 0] > EXP2_ZERO_BELOW).astype(jnp.int32)

    def finish(j):
        def cond(state):
            kb, go = state
            return jnp.logical_and(kb >= 0, go > 0)

        def body(state):
            kb, _ = state
            out = []
            run_staged([visit(stacked_q(j), kb, 1, False, carry_scr[j], acc_scr[j], out)])
            carry_scr[j], acc_scr[j] = out[0]
            return kb - 1, alive(carry_scr[j])

        lax.while_loop(cond, body, (first_q + j - lead, alive(carry_scr[j])))

    @pl.when(alive(top_scr[...]) > 0)
    def _():
        for j in range(n_q):
            finish(j)

    for j in range(n_q):
        o_ref[j * blk:(j + 1) * blk, :] = jnp.where(
            lane_head == 0, acc_scr[j, :blk], acc_scr[j, blk:]).astype(BF16)


def _stick_breaking(q, k, v, *, bsz, seq):
    tokens, width = q.shape
    blk = Q_BLOCK
    n_q = STICK_Q_BLOCKS_PER_STEP
    assert n_q >= STICK_LEAD_BLOCKS - 1
    steps = seq // (blk * n_q)
    j_idx = lax.broadcasted_iota(jnp.int32, (2 * blk, 2 * blk), 0) % blk
    s_idx = lax.broadcasted_iota(jnp.int32, (2 * blk, 2 * blk), 1)
    later = jnp.where((j_idx > s_idx) | (s_idx >= blk), 1.0, 0.0).astype(BF16)
    q_spec = pl.BlockSpec((n_q * blk, LANES), lambda b, p, i: (b * steps + i, p))
    kv_spec = pl.BlockSpec((seq, LANES), lambda b, p, i: (b, p))
    return pl.pallas_call(
        functools.partial(_stick_kernel, blk=blk, lead=STICK_LEAD_BLOCKS, n_q=n_q),
        grid=(bsz, width // LANES, steps),
        in_specs=[q_spec, kv_spec, kv_spec, pl.BlockSpec(later.shape, lambda b, p, i: (0, 0))],
        out_specs=q_spec,
        out_shape=jax.ShapeDtypeStruct((tokens, width), BF16),
        scratch_shapes=[pltpu.VMEM((n_q, HEADS_PER_TILE * blk, LANES), F32),
                        pltpu.VMEM((n_q, HEADS_PER_TILE * blk, blk), F32),
                        pltpu.VMEM((8, blk), F32)],
        compiler_params=_params("parallel", "parallel", "parallel"),
        name="stick_breaking",
    )(q, k, v, later)


def _window_kernel(q_ref, k_ref, v_ref, o_ref, lse_ref, *, seq, max_tq):
    pair = pl.program_id(1)
    rows = HEADS_PER_TILE * WINDOW
    lane_head = lax.broadcasted_iota(jnp.int32, (WINDOW, LANES), 1) // HEAD_DIM
    off = (lax.broadcasted_iota(jnp.int32, (rows, 2 * WINDOW), 1)
           - lax.broadcasted_iota(jnp.int32, (rows, 2 * WINDOW), 0) % WINDOW)
    band = (off >= 0) & (off <= WINDOW)
    from_prev = lax.broadcasted_iota(jnp.int32, (rows, 2 * WINDOW), 1) < WINDOW

    def class_rows(first, d):
        return pl.ds(first, WINDOW) if d == 1 else pl.ds(first, WINDOW, stride=d)

    def block(d, r, j, prev, out):
        cur = class_rows(r + d * j, d)
        q = q_ref[cur, :].astype(BF16)
        k_cur = k_ref[cur, :].astype(BF16)
        v_cur = v_ref[cur, :].astype(BF16)
        out.append((k_cur, v_cur))
        if prev is None:
            before = class_rows(r + d * jnp.maximum(j - WINDOW, 0), d)
            prev = (k_ref[before, :].astype(BF16), v_ref[before, :].astype(BF16))
            valid = band & (jnp.logical_not(from_prev) | (j > 0))
        else:
            valid = band
        q_stack = jnp.concatenate(
            [jnp.where(lane_head == h, q, jnp.zeros_like(q)) for h in range(HEADS_PER_TILE)], axis=0)
        z = lax.dot_general(q_stack, jnp.concatenate([prev[0], k_cur], axis=0),
                            (((1,), (1,)), ((), ())), preferred_element_type=F32)
        yield
        z = jnp.where(valid, z, -jnp.inf)
        z_max = jnp.max(z, axis=-1, keepdims=True)
        prob = jnp.exp(z - z_max)
        denom = jnp.sum(prob, axis=-1, keepdims=True)
        pv = jnp.dot(prob.astype(BF16), jnp.concatenate([prev[1], v_cur], axis=0),
                     preferred_element_type=F32)
        yield
        o = pv * (1.0 / denom)
        lse = z_max + jnp.log(denom)
        o_ref[cur, :] = jnp.where(lane_head == 0, o[:WINDOW], o[WINDOW:])
        lse_ref[cur, :] = jnp.where(lane_head == 0, lse[:WINDOW], lse[WINDOW:])

    def tiles(d):
        tq = min(max_tq, seq // d)
        runs = max_tq // tq
        per_class = seq // d // tq

        def tile(i, _):
            visits = []
            for c in range(runs):
                run = i * runs + c
                r = run // per_class
                j0 = (run % per_class) * tq
                prev = None
                for s in range(tq // WINDOW):
                    out = []
                    visits.append(block(d, r, j0 + s * WINDOW, prev, out))
                    next(visits[-1])
                    prev = out[0]
            while visits:
                visits = [g for g in visits if next(g, True) is None]
            return 0

        lax.fori_loop(0, seq // max_tq, tile, 0)

    for idx, d in enumerate(DILATIONS):
        pl.when(pair == idx)(functools.partial(tiles, d))


def _dilated_window(q, k, v, *, bsz, seq):
    spec = pl.BlockSpec((None, seq, LANES), lambda b, p: (p, b, 0))
    return pl.pallas_call(
        functools.partial(_window_kernel, seq=seq, max_tq=WINDOW_TILE_ROWS),
        grid=(bsz, q.shape[0]),
        in_specs=[spec] * 3,
        out_specs=[spec] * 2,
        out_shape=[jax.ShapeDtypeStruct(q.shape, F32)] * 2,
        compiler_params=_params("parallel", "parallel"),
        name="dilated_window",
    )(q, k, v)


def _layer_tail_kernel(x_ref, mod_ref, ya_ref, yb_ref, oc_ref, lse_ref, wout_ref, g_ref, wup_ref,
                       cw_ref, cb_ref, wdn_ref, gfin_ref, *rest, tm, tn, per_batch, final):
    if final:
        o_ref, y_scr, x_scr, tail_scr, h_scr, up_scr, act_scr, down_scr = rest
    else:
        win_ref, o_ref, win_bf_ref, y_scr, x_scr, tail_scr, h_scr, up_scr, act_scr, down_scr = rest
        win_bf_ref[...] = win_ref[...].astype(BF16)
    i = pl.program_id(0)
    mod = mod_ref[0]
    d_ff = wdn_ref.shape[0]

    y_scr[:, :A_WIDTH] = ya_ref[...]
    y_scr[:, A_WIDTH:A_WIDTH + B_WIDTH] = yb_ref[...]
    pairs = C_WIDTH // LANES
    lse = [lse_ref[p] for p in range(pairs)]
    top = functools.reduce(jnp.maximum, lse)
    share = [jnp.exp(l - top) for l in lse]
    inv = 1.0 / functools.reduce(lambda a, b: a + b, share)
    c0 = A_WIDTH + B_WIDTH
    for p in range(pairs):
        y_scr[:, c0 + p * LANES:c0 + (p + 1) * LANES] = (oc_ref[p] * (share[p] * inv)).astype(BF16)
    mixed = jnp.dot(y_scr[...], wout_ref[...], preferred_element_type=F32)
    x_scr[...] = x_ref[...] + (1.0 + mod[2:3]) * mixed

    def norm_mod(xv):
        return _rms_scale(xv) * g_ref[...] * (1.0 + mod[4:5]) + mod[3:4]

    opens_sequence = i % per_batch == 0

    @pl.when(opens_sequence)
    def _():
        tail_scr[...] = jnp.zeros_like(tail_scr)

    halo = jnp.where(opens_sequence, 0.0, norm_mod(tail_scr[...]))
    h_scr[:CONV_HALO, :] = halo.astype(BF16)
    h_scr[CONV_HALO:, :] = norm_mod(x_scr[...]).astype(BF16)
    tail_scr[...] = x_scr[tm - CONV_HALO:, :]

    half = tm // 2

    slabs = range(tn // LANES)
    n_tiles = d_ff // tn
    in_flight = up_scr.shape[0] // 2

    def project(n):
        for part in range(2):
            c0 = part * d_ff + n * tn
            up = jnp.dot(h_scr[...], wup_ref[:, c0:c0 + tn], preferred_element_type=F32)
            for s in slabs:
                up_scr[2 * (n % in_flight) + part, s] = up[:, s * LANES:(s + 1) * LANES]

    def conv(slot, c0):
        w = cw_ref[:, c0:c0 + tn]
        bias = cb_ref[:, c0:c0 + tn]
        parities = []
        for parity in range(2):
            cols = []
            for s in slabs:
                lanes = slice(s * LANES, (s + 1) * LANES)
                out = bias[:, lanes]
                for tap in range(CONV_WIDTH):
                    first = CONV_HALO - (CONV_WIDTH - 1) + tap + parity
                    out = out + w[tap:tap + 1, lanes] * up_scr[slot, s, pl.ds(first, half, stride=2), :]
                cols.append(out)
            parities.append(jnp.concatenate(cols, axis=1))
        return jnp.concatenate(parities, axis=0)

    project(0)
    for n in range(n_tiles):
        if n + 1 < n_tiles:
            project(n + 1)
        slot = 2 * (n % in_flight)
        gate = conv(slot, n * tn)
        val = conv(slot + 1, d_ff + n * tn)
        act = gate * (1.0 / (1.0 + jnp.exp(-gate))) * val
        act_scr[:, n * tn:(n + 1) * tn] = act.astype(BF16)
    down = jnp.dot(act_scr[...], wdn_ref[...], preferred_element_type=F32)
    for s in range(down_scr.shape[0]):
        lanes = slice(s * LANES, (s + 1) * LANES)
        down_scr[s, pl.ds(0, half, stride=2), :] = down[:half, lanes]
        down_scr[s, pl.ds(1, half, stride=2), :] = down[half:, lanes]
    down = jnp.concatenate([down_scr[s] for s in range(down_scr.shape[0])], axis=1)
    x_new = x_scr[...] + (1.0 + mod[5:6]) * down
    if final:
        x_new = _rms_scale(x_new) * gfin_ref[...]
    o_ref[...] = x_new


def _layer_tail(x2d, mod_l, y_a, y_b, o_c, lse, w_out, g_ffn, w_up, conv_w, conv_b, w_down, g_final,
                w_in_all, *, layer, seq, tm, tn, final):
    tokens, d = x2d.shape
    per_batch = seq // tm
    tile = lambda w: pl.BlockSpec((tm, w), lambda i: (i, 0))
    full = lambda a: pl.BlockSpec(a.shape, lambda i: (0,) * a.ndim)
    g_ffn = g_ffn.reshape(1, d)
    conv_b = conv_b.reshape(1, -1)
    g_final = g_final.reshape(1, d)
    steps = tokens // tm
    cast = [] if final else [_cast_specs(w_in_all, layer + 1, steps)]
    outs = pl.pallas_call(
        functools.partial(_layer_tail_kernel, tm=tm, tn=tn, per_batch=per_batch, final=final),
        grid=(steps,),
        in_specs=[tile(d),
                  pl.BlockSpec((1, N_MOD, d), lambda i: (i // per_batch, 0, 0)),
                  tile(A_WIDTH), tile(B_WIDTH), _pair_major_tile(tm), _pair_major_tile(tm),
                  _resident(w_out),
                  full(g_ffn), _resident(w_up), full(conv_w), full(conv_b), _resident(w_down),
                  full(g_final)] + [c[0] for c in cast],
        out_specs=[tile(d)] + [c[1] for c in cast],
        out_shape=[jax.ShapeDtypeStruct((tokens, d), F32)] + [c[2] for c in cast],
        scratch_shapes=[pltpu.VMEM((tm, w_out.shape[0]), BF16),
                        pltpu.VMEM((tm, d), F32),
                        pltpu.VMEM((CONV_HALO, d), F32),
                        pltpu.VMEM((tm + CONV_HALO, d), BF16),
                        pltpu.VMEM((4, tn // LANES, tm + CONV_HALO, LANES), F32),
                        pltpu.VMEM((tm, w_down.shape[0]), BF16),
                        pltpu.VMEM((d // LANES, tm, LANES), F32)],
        compiler_params=_params("arbitrary"),
        name="layer_tail",
    )(x2d, mod_l, y_a, y_b, o_c, lse, w_out, g_ffn, w_up, conv_w, conv_b, w_down, g_final,
      *([] if final else [w_in_all]))
    return outs[0], (None if final else outs[1])


def kernel(x, c, positions, w_ada, b_ada, g_mix, w_in, g_sgu, w_sp, b_sp, w_out, g_ffn, w_up,
           conv_w, conv_b, w_down, g_final):
    bsz, seq, d = x.shape
    depth = w_ada.shape[0]
    tm = min(512, seq)
    assert seq % tm == 0 and seq % (max(DILATIONS) * WINDOW) == 0
    mod = _modulation(c, w_ada, b_ada)
    cos, sin = _rope_tables(positions)
    cos = cos.reshape(bsz * seq, LANES)
    sin = sin.reshape(bsz * seq, LANES)
    x2d = x.reshape(bsz * seq, d)
    w_in_l = w_in[0].astype(BF16)
    for l in range(depth):
        (y_a, b_q, b_k, b_v, c_q, c_k, c_v), (w_out_l, w_up_l, w_down_l) = _in_projection(
            x2d, mod[l], g_mix[l], w_in_l, cos, sin, g_sgu[l], w_sp[l], b_sp[l],
            (w_out, w_up, w_down), layer=l, seq=seq, tm=min(1024, seq))
        y_b = _stick_breaking(b_q, b_k, b_v, bsz=bsz, seq=seq)
        o_c, lse = _dilated_window(c_q, c_k, c_v, bsz=bsz, seq=seq)
        x2d, w_in_l = _layer_tail(x2d, mod[l], y_a, y_b, o_c, lse, w_out_l, g_ffn[l], w_up_l,
                                  conv_w[l], conv_b[l], w_down_l, g_final, w_in, layer=l, seq=seq,
                                  tm=tm, tn=256, final=(l == depth - 1))
    return x2d.reshape(bsz, seq, d)
```
